```python
import jax
import jax.numpy as jnp
from jax import lax
import numpy as np


D_MODEL = 1024
BATCH = 4
SEQ = 8192
DEPTH = 2

GRID_W = 64
CTX_LEN = 256
HEAD_DIM = 64
D_MIX = D_MODEL
GROUP_WIDTH = D_MIX // 4
A_HQ = GROUP_WIDTH // HEAD_DIM
A_HKV = A_HQ // 2
A_WIN = 128
A_BLK = 128
B_H = GROUP_WIDTH // HEAD_DIM
NAT_KR_MAX = 8
NAT_KC = 16
NAT_QC = 16
NAT_KCB = 2 * NAT_KC
C_H = GROUP_WIDTH // HEAD_DIM
C_CHUNK = 64
C_CONV = 3
D_H = GROUP_WIDTH // HEAD_DIM
MLA_Q_LORA = 192
MLA_KV_LORA = 128
MLA_NOPE = 64
MLA_ROPE = 32
MLA_V = GROUP_WIDTH // D_H
MLA_BLK = 128
N_EXP = 32
TOP_K = 4
D_FF = D_MODEL
SWIGLU_LIMIT = 7.0
SWIGLU_ALPHA = 1.702
MOE_BLK = 128
ROPE_BASE = 10000.0
EPS = 1e-6
IN_SIZES = (A_HQ * HEAD_DIM, A_HKV * HEAD_DIM, A_HKV * HEAD_DIM,
            B_H * HEAD_DIM, B_H * HEAD_DIM, B_H * HEAD_DIM,
            2 * C_H * HEAD_DIM, C_H * HEAD_DIM, C_H * HEAD_DIM, 4 * C_H,
            MLA_Q_LORA, MLA_KV_LORA, MLA_ROPE)
IN_COLS = sum(IN_SIZES)

kernel_name = 'hybrid_parallel_heads_dit_block'


def rms_norm(x, g):
    xf = x.astype(jnp.float32)
    y = xf * lax.rsqrt(jnp.mean(xf * xf, axis=-1, keepdims=True) + EPS)
    return (y * g.astype(jnp.float32)).astype(x.dtype)


def split_cols(p, sizes):
    return jnp.split(p, np.cumsum(sizes)[:-1].tolist(), axis=-1)


def axial_rope_tables(n, rot_dim):
    t = jnp.arange(n, dtype=jnp.int32)
    rows = (t // GRID_W).astype(jnp.float32)
    cols = (t % GRID_W).astype(jnp.float32)
    d_ax = rot_dim // 2
    inv = ROPE_BASE ** (-jnp.arange(0, d_ax, 2, dtype=jnp.float32) / d_ax)
    ang = jnp.concatenate([rows[:, None] * inv, cols[:, None] * inv], axis=-1)
    return jnp.cos(ang), jnp.sin(ang)


def apply_rope(x, cos, sin):
    half = x.shape[-1] // 2
    x1 = x[..., :half].astype(jnp.float32)
    x2 = x[..., half:].astype(jnp.float32)
    c = cos[None, :, None, :]
    s = sin[None, :, None, :]
    return jnp.concatenate([x1 * c - x2 * s, x2 * c + x1 * s], axis=-1).astype(x.dtype)


def ctx_self_attention(qc, kc, vc, sink=None):
    bn, n, hq, dq = qc.shape
    hkv = kc.shape[2]
    g = hq // hkv
    s = jnp.einsum('bqhgd,bkhd->bhgqk', qc.reshape(bn, n, hkv, g, dq), kc).astype(jnp.float32) * dq ** -0.5
    if sink is not None:
        snk = jnp.broadcast_to(sink.astype(jnp.float32).reshape(1, hkv, g, 1, 1), s.shape[:-1] + (1,))
        s = jnp.concatenate([s, snk], axis=-1)
    p = jax.nn.softmax(s, axis=-1)[..., :n].astype(vc.dtype)
    return jnp.einsum('bhgqk,bkhd->bqhgd', p, vc).reshape(bn, n, hq * vc.shape[-1])


def mixer_a(pl, pc, sink, cos, sin, need_ctx):
    bn, S = pl[0].shape[:2]
    nc = pc[0].shape[1]
    g = A_HQ // A_HKV
    nb = S // A_BLK
    kw = 3 * A_BLK
    scale = HEAD_DIM ** -0.5
    q = apply_rope(pl[0].reshape(bn, S, A_HQ, HEAD_DIM), cos, sin)
    k = apply_rope(pl[1].reshape(bn, S, A_HKV, HEAD_DIM), cos, sin)
    v = pl[2].reshape(bn, S, A_HKV, HEAD_DIM)
    qc = pc[0].reshape(bn, nc, A_HQ, HEAD_DIM)
    kc = pc[1].reshape(bn, nc, A_HKV, HEAD_DIM)
    vc = pc[2].reshape(bn, nc, A_HKV, HEAD_DIM)
    qb = q.reshape(bn, nb, A_BLK, A_HKV, g, HEAD_DIM)

    def band(a):
        ap = jnp.pad(a, ((0, 0), (A_BLK, A_BLK), (0, 0), (0, 0))).reshape(bn, nb + 2, A_BLK, A_HKV, HEAD_DIM)
        return jnp.concatenate([ap[:, :-2], ap[:, 1:-1], ap[:, 2:]], axis=2)

    kb, vb = band(k), band(v)
    qpos = jnp.arange(S).reshape(nb, A_BLK)
    kpos = (jnp.arange(nb)[:, None] - 1) * A_BLK + jnp.arange(kw)[None, :]
    ok = ((jnp.abs(qpos[:, :, None] - kpos[:, None, :]) <= A_WIN)
          & (kpos[:, None, :] >= 0) & (kpos[:, None, :] < S))
    s_loc = jnp.einsum('bnqhgd,bnkhd->bnhgqk', qb, kb).astype(jnp.float32) * scale
    s_loc = jnp.where(ok[None, :, None, None], s_loc, -jnp.inf)
    s_ctx = jnp.einsum('bnqhgd,bchd->bnhgqc', qb, kc).astype(jnp.float32) * scale
    s_snk = jnp.broadcast_to(sink.astype(jnp.float32).reshape(1, 1, A_HKV, g, 1, 1), s_ctx.shape[:-1] + (1,))
    p = jax.nn.softmax(jnp.concatenate([s_loc, s_ctx, s_snk], axis=-1), axis=-1).astype(v.dtype)
    o = (jnp.einsum('bnhgqk,bnkhd->bnqhgd', p[..., :kw], vb)
         + jnp.einsum('bnhgqc,bchd->bnqhgd', p[..., kw:kw + nc], vc))
    o = o.reshape(bn, S, A_HQ * HEAD_DIM)
    oc = ctx_self_attention(qc, kc, vc, sink) if need_ctx else None
    return o, oc


def mixer_b(pl, pc, rpb, need_ctx):
    bn, S = pl[0].shape[:2]
    heads = lambda a: a.reshape(a.shape[0], a.shape[1], B_H, HEAD_DIM)
    q, k, v = heads(pl[0]), heads(pl[1]), heads(pl[2])
    qc, kc, vc = heads(pc[0]), heads(pc[1]), heads(pc[2])
    rows = S // GRID_W
    kr = min(NAT_KR_MAX, rows)
    ncb = GRID_W // NAT_QC
    nk = kr * NAT_KCB
    scale = HEAD_DIM ** -0.5
    qcol = jnp.arange(ncb)[:, None] * NAT_QC + jnp.arange(NAT_QC)[None, :]
    cs = jnp.clip(qcol - NAT_KC // 2, 0, GRID_W - NAT_KC)
    a0 = jnp.clip(jnp.arange(ncb) * NAT_QC - NAT_KC // 2, 0, GRID_W - NAT_KCB)
    krow_off = jnp.repeat(jnp.arange(kr), NAT_KCB)
    kcol = (a0[:, None, None] + jnp.zeros((1, kr, 1), jnp.int32)
            + jnp.arange(NAT_KCB)[None, None, :]).reshape(ncb, nk)
    col_ok = (kcol[:, None, :] >= cs[:, :, None]) & (kcol[:, None, :] < cs[:, :, None] + NAT_KC)
    dc_idx = jnp.clip(kcol[:, None, :] - qcol[:, :, None] + NAT_KC - 1, 0, 2 * NAT_KC - 2)
    q_rows = jnp.moveaxis(q.reshape(bn, rows, ncb, NAT_QC, B_H, HEAD_DIM), 1, 0)

    def row_block(args):
        qr, r = args
        r0 = jnp.clip(r - kr // 2, 0, rows - kr)
        krow = r0 + krow_off
        kidx = krow[None, :] * GRID_W + kcol
        kg, vg = k[:, kidx], v[:, kidx]
        dr_idx = jnp.broadcast_to(krow - r + NAT_KR_MAX - 1, dc_idx.shape)
        bias = rpb[:, dr_idx, dc_idx].astype(jnp.float32)
        s_loc = jnp.einsum('bmqhd,bmkhd->bhmqk', qr, kg).astype(jnp.float32) * scale + bias[None]
        s_loc = jnp.where(col_ok[None, None], s_loc, -jnp.inf)
        s_ctx = jnp.einsum('bmqhd,bchd->bhmqc', qr, kc).astype(jnp.float32) * scale
        p = jax.nn.softmax(jnp.concatenate([s_loc, s_ctx], axis=-1), axis=-1).astype(v.dtype)
        return (jnp.einsum('bhmqk,bmkhd->bmqhd', p[..., :nk], vg)
                + jnp.einsum('bhmqc,bchd->bmqhd', p[..., nk:], vc))

    o = lax.map(row_block, (q_rows, jnp.arange(rows)))
    o = jnp.moveaxis(o, 0, 1).reshape(bn, S, B_H * HEAD_DIM)
    oc = ctx_self_attention(qc, kc, vc) if need_ctx else None
    return o, oc


def short_conv_silu(u, w):
    K = w.shape[0]
    p = K // 2
    n = u.shape[1]
    up = jnp.pad(u, ((0, 0), (p, p), (0, 0)))
    return jax.nn.silu(sum(up[:, j:j + n] * w[j] for j in range(K)))


def mlstm_chunkwise(q, k, v, ig, lf, state):
    bn, h, n, d = q.shape
    L = C_CHUNK
    nc = n // L
    to_chunks = lambda a: jnp.moveaxis(a.reshape(a.shape[:2] + (nc, L) + a.shape[3:]), 2, 0)
    tril = jnp.tril(jnp.ones((L, L), dtype=bool))

    def step(carry, inp):
        cmat, nvec, m = carry
        qc, kc, vc, ic, fc = inp
        b = jnp.cumsum(fc, axis=-1)
        logd = jnp.where(tril, b[..., :, None] - b[..., None, :] + ic[..., None, :], -jnp.inf)
        m_t = jnp.maximum(b + m[..., None], jnp.max(logd, axis=-1))
        dmat = jnp.exp(logd - m_t[..., None])
        inter = jnp.exp(b + m[..., None] - m_t)
        sw = jnp.einsum('bhtd,bhsd->bhts', qc, kc) * dmat
        num = jnp.einsum('bhts,bhsd->bhtd', sw, vc) + inter[..., None] * jnp.einsum('bhtk,bhkv->bhtv', qc, cmat)
        den = jnp.sum(sw, axis=-1) + inter * jnp.einsum('bhtk,bhk->bht', qc, nvec)
        hout = num / jnp.maximum(jnp.abs(den), jnp.exp(-m_t))[..., None]
        b_end = b[..., -1]
        m_new = m_t[..., -1]
        decay = jnp.exp(b_end + m - m_new)
        w = jnp.exp(b_end[..., None] - b + ic - m_new[..., None])
        c_new = decay[..., None, None] * cmat + jnp.einsum('bhs,bhsk,bhsv->bhkv', w, kc, vc)
        n_new = decay[..., None] * nvec + jnp.einsum('bhs,bhsk->bhk', w, kc)
        return (c_new, n_new, m_new), hout

    state, hs = lax.scan(step, state, (to_chunks(q), to_chunks(k), to_chunks(v), to_chunks(ig), to_chunks(lf)))
    return jnp.moveaxis(hs, 0, 2).reshape(bn, h, n, d), state


def mixer_c(pl, pc, conv_w, gate_bias, need_ctx):
    def prep(qk, v, o, g):
        bn, n = qk.shape[:2]
        q, k = jnp.split(short_conv_silu(qk, conv_w), 2, axis=-1)
        heads = lambda a: a.reshape(bn, n, C_H, HEAD_DIM).transpose(0, 2, 1, 3).astype(jnp.float32)
        g = (g.reshape(bn, n, 4, C_H).astype(jnp.float32) + gate_bias.astype(jnp.float32)).transpose(2, 0, 3, 1)
        fwd = (g[0], jax.nn.log_sigmoid(g[1]))
        bwd = (g[2], jax.nn.log_sigmoid(g[3]))
        return heads(q), heads(k) * HEAD_DIM ** -0.5, heads(v), jax.nn.sigmoid(o), fwd, bwd

    def bidir(q, k, v, fwd, bwd, st_f, st_b):
        hf, st_f = mlstm_chunkwise(q, k, v, fwd[0], fwd[1], st_f)
        flip = lambda a: jnp.flip(a, axis=2)
        hb, st_b = mlstm_chunkwise(flip(q), flip(k), flip(v), flip(bwd[0]), flip(bwd[1]), st_b)
        return hf + flip(hb), st_f, st_b

    def merge(hsum, o):
        return (o.astype(jnp.float32) * hsum.transpose(0, 2, 1, 3).reshape(o.shape)).astype(o.dtype)

    qc, kc, vc, oc, fc, bc = prep(*pc)
    bn = qc.shape[0]
    zero = (jnp.zeros((bn, C_H, HEAD_DIM, HEAD_DIM), jnp.float32),
            jnp.zeros((bn, C_H, HEAD_DIM), jnp.float32),
            jnp.zeros((bn, C_H), jnp.float32))
    hc, st_f, st_b = bidir(qc, kc, vc, fc, bc, zero, zero)
    q, k, v, o, fw, bw = prep(*pl)
    hl, _, _ = bidir(q, k, v, fw, bw, st_f, st_b)
    return merge(hl, o), (merge(hc, oc) if need_ctx else None)


def mixer_d(pl, pc, g_q, w_uq, g_kv, w_ukv, cos, sin, need_ctx):
    dq = MLA_NOPE + MLA_ROPE
    scale = dq ** -0.5

    def expand(cq, ckv, kr, rope):
        bn, n = cq.shape[:2]
        qf = (rms_norm(cq, g_q) @ w_uq).reshape(bn, n, D_H, dq)
        kv = (rms_norm(ckv, g_kv) @ w_ukv).reshape(bn, n, D_H, MLA_NOPE + MLA_V)
        q_nope, q_rope = qf[..., :MLA_NOPE], qf[..., MLA_NOPE:]
        k_nope, v = kv[..., :MLA_NOPE], kv[..., MLA_NOPE:]
        k_rope = kr[:, :, None, :]
        if rope:
            q_rope = apply_rope(q_rope, cos, sin)
            k_rope = apply_rope(k_rope, cos, sin)
        q = jnp.concatenate([q_nope, q_rope], axis=-1)
        k = jnp.concatenate([k_nope, jnp.broadcast_to(k_rope, (bn, n, D_H, MLA_ROPE))], axis=-1)
        return q, k, v

    q, k, v = expand(pl[0], pl[1], pl[2], True)
    qc, kc, vc = expand(pc[0], pc[1], pc[2], False)
    bn, S = q.shape[:2]
    nb = S // MLA_BLK
    k_all = jnp.concatenate([k, kc], axis=1)
    v_all = jnp.concatenate([v, vc], axis=1)
    qb = jnp.moveaxis(q.reshape(bn, nb, MLA_BLK, D_H, dq), 1, 0)

    def q_block(qi):
        s = jnp.einsum('bqhd,bkhd->bhqk', qi, k_all).astype(jnp.float32) * scale
        p = jax.nn.softmax(s, axis=-1).astype(v_all.dtype)
        return jnp.einsum('bhqk,bkhd->bqhd', p, v_all)

    o = jnp.moveaxis(lax.map(q_block, qb), 0, 1).reshape(bn, S, D_H * MLA_V)
    oc = ctx_self_attention(qc, kc, vc) if need_ctx else None
    return o, oc


def moe_ffn(h, w_router, b_router, w_gu, b_gu, w_down, b_down):
    T, dm = h.shape
    n_assign = T * TOP_K
    n_blk = (n_assign + N_EXP * (MOE_BLK - 1) + MOE_BLK - 1) // MOE_BLK
    cap = n_blk * MOE_BLK
    logits = (h @ w_router + b_router).astype(jnp.float32)
    top_logit, top_exp = lax.top_k(logits, TOP_K)
    gate = jax.nn.softmax(top_logit, axis=-1).reshape(-1)
    exp_id = top_exp.reshape(-1)
    tok_id = jnp.arange(n_assign, dtype=jnp.int32) // TOP_K
    order = jnp.argsort(exp_id)
    sorted_exp = exp_id[order]
    counts = jnp.bincount(exp_id, length=N_EXP)
    padded = (counts + MOE_BLK - 1) // MOE_BLK * MOE_BLK
    start = jnp.cumsum(counts) - counts
    pad_end = jnp.cumsum(padded)
    pad_start = pad_end - padded
    dest = pad_start[sorted_exp] + jnp.arange(n_assign, dtype=jnp.int32) - start[sorted_exp]
    slot_tok = jnp.full((cap,), T, jnp.int32).at[dest].set(tok_id[order])
    slot_gate = jnp.zeros((cap,), jnp.float32).at[dest].set(gate[order])
    blk_exp = jnp.minimum(jnp.searchsorted(pad_end, jnp.arange(n_blk, dtype=jnp.int32) * MOE_BLK, side='right'), N_EXP - 1)
    h_pad = jnp.concatenate([h, jnp.zeros((1, dm), h.dtype)], axis=0)

    def expert_block(args):
        tok, e = args
        gu = h_pad[tok] @ w_gu[e] + b_gu[e]
        g = jnp.minimum(gu[:, 0::2], SWIGLU_LIMIT)
        u = jnp.clip(gu[:, 1::2], -SWIGLU_LIMIT, SWIGLU_LIMIT)
        return ((u + 1.0) * g * jax.nn.sigmoid(SWIGLU_ALPHA * g)) @ w_down[e] + b_down[e]

    y = lax.map(expert_block, (slot_tok.reshape(n_blk, MOE_BLK), blk_exp))
    y = y.reshape(cap, dm) * slot_gate[:, None].astype(y.dtype)
    return jnp.zeros((T + 1, dm), y.dtype).at[slot_tok].add(y)[:T]


def setup_inputs(seed: int = 0) -> dict:
    key = jax.random.key(seed)
    ks = jax.random.split(key, 32)
    f32 = jnp.float32
    nrm = lambda k, shape, s: jax.random.normal(k, shape, f32) * s
    D = D_MODEL
    f_bias = jnp.linspace(3.0, 6.0, C_H, dtype=f32)
    gate_bias = nrm(ks[9], (DEPTH, 4, C_H), 0.1) + jnp.array([0.0, 1.0, 0.0, 1.0], f32)[None, :, None] * f_bias[None, None, :]
    return {
        'x': nrm(ks[0], (BATCH, SEQ, D), 1.0),
        'c': nrm(ks[1], (BATCH, D), 1.0),
        'ctx': nrm(ks[2], (BATCH, CTX_LEN, D), 1.0),
        'c_ctx': nrm(ks[3], (D,), 1.0),
        'w_ada': nrm(ks[4], (DEPTH, D, 6 * D), 0.5 * D ** -0.5),
        'b_ada': nrm(ks[5], (DEPTH, 6 * D), 0.01),
        'g_norm1': 1.0 + nrm(ks[6], (DEPTH, D), 0.01),
        'g_norm2': 1.0 + nrm(ks[7], (DEPTH, D), 0.01),
        'w_in': nrm(ks[8], (DEPTH, D, IN_COLS), D ** -0.5),
        'attn_sink': nrm(ks[10], (DEPTH, A_HQ), 0.5),
        'nat_rpb': nrm(ks[11], (DEPTH, B_H, 2 * NAT_KR_MAX - 1, 2 * NAT_KC - 1), 0.1),
        'mlstm_conv': nrm(ks[12], (DEPTH, C_CONV, 2 * C_H * HEAD_DIM), C_CONV ** -0.5),
        'mlstm_gate_bias': gate_bias,
        'mla_g_q': 1.0 + nrm(ks[13], (DEPTH, MLA_Q_LORA), 0.01),
        'mla_w_uq': nrm(ks[14], (DEPTH, MLA_Q_LORA, D_H * (MLA_NOPE + MLA_ROPE)), MLA_Q_LORA ** -0.5),
        'mla_g_kv': 1.0 + nrm(ks[15], (DEPTH, MLA_KV_LORA), 0.01),
        'mla_w_ukv': nrm(ks[16], (DEPTH, MLA_KV_LORA, D_H * (MLA_NOPE + MLA_V)), MLA_KV_LORA ** -0.5),
        'w_out': nrm(ks[17], (DEPTH, D_MIX, D), D_MIX ** -0.5),
        'w_router': nrm(ks[18], (DEPTH, D, N_EXP), D ** -0.5),
        'b_router': nrm(ks[19], (DEPTH, N_EXP), 0.01),
        'w_gu': nrm(ks[20], (DEPTH, N_EXP, D, 2 * D_FF), D ** -0.5),
        'b_gu': nrm(ks[21], (DEPTH, N_EXP, 2 * D_FF), 0.01),
        'w_down': nrm(ks[22], (DEPTH, N_EXP, D_FF, D), D_FF ** -0.5),
        'b_down': nrm(ks[23], (DEPTH, N_EXP, D), 0.01),
        'g_final': 1.0 + nrm(ks[24], (D,), 0.01),
    }


def reference(x, c, ctx, c_ctx, w_ada, b_ada, g_norm1, g_norm2, w_in, attn_sink, nat_rpb,
              mlstm_conv, mlstm_gate_bias, mla_g_q, mla_w_uq, mla_g_kv, mla_w_ukv, w_out,
              w_router, b_router, w_gu, b_gu, w_down, b_down, g_final):
    S = x.shape[1]
    dm = x.shape[-1]
    cos_a, sin_a = axial_rope_tables(S, HEAD_DIM)
    cos_d, sin_d = axial_rope_tables(S, MLA_ROPE)
    silu_c = jax.nn.silu(c)
    silu_cc = jax.nn.silu(c_ctx)
    xc = ctx
    for l in range(DEPTH):
        need_ctx = l < DEPTH - 1
        mod = silu_c @ w_ada[l] + b_ada[l]
        mod_c = silu_cc @ w_ada[l] + b_ada[l]
        sh1, sc1, gt1, sh2, sc2, gt2 = jnp.split(mod[:, None, :], 6, axis=-1)
        sh1c, sc1c, gt1c, sh2c, sc2c, gt2c = jnp.split(mod_c, 6, axis=-1)
        h = rms_norm(x, g_norm1[l]) * (1.0 + sc1) + sh1
        hc = rms_norm(xc, g_norm1[l]) * (1.0 + sc1c) + sh1c
        pl = split_cols(h @ w_in[l], IN_SIZES)
        pc = split_cols(hc @ w_in[l], IN_SIZES)
        ya, yac = mixer_a(pl[0:3], pc[0:3], attn_sink[l], cos_a, sin_a, need_ctx)
        yb, ybc = mixer_b(pl[3:6], pc[3:6], nat_rpb[l], need_ctx)
        ym, ymc = mixer_c(pl[6:10], pc[6:10], mlstm_conv[l], mlstm_gate_bias[l], need_ctx)
        yd, ydc = mixer_d(pl[10:13], pc[10:13], mla_g_q[l], mla_w_uq[l], mla_g_kv[l], mla_w_ukv[l], cos_d, sin_d, need_ctx)
        x = x + gt1 * (jnp.concatenate([ya, yb, ym, yd], axis=-1) @ w_out[l])
        h2 = rms_norm(x, g_norm2[l]) * (1.0 + sc2) + sh2
        moe_w = (w_router[l], b_router[l], w_gu[l], b_gu[l], w_down[l], b_down[l])
        if need_ctx:
            xc = xc + gt1c * (jnp.concatenate([yac, ybc, ymc, ydc], axis=-1) @ w_out[l])
            h2c = rms_norm(xc, g_norm2[l]) * (1.0 + sc2c) + sh2c
            n_lat = h2.shape[0] * h2.shape[1]
            f = moe_ffn(jnp.concatenate([h2.reshape(-1, dm), h2c.reshape(-1, dm)], axis=0), *moe_w)
            x = x + gt2 * f[:n_lat].reshape(x.shape)
            xc = xc + gt2c * f[n_lat:].reshape(xc.shape)
        else:
            x = x + gt2 * moe_ffn(h2.reshape(-1, dm), *moe_w).reshape(x.shape)
    return rms_norm(x, g_final)
```

```python
import functools

import jax
import jax.numpy as jnp
from jax import lax
from jax.experimental import pallas as pl
from jax.experimental.pallas import tpu as pltpu

F32 = jnp.float32
MXU = jnp.bfloat16

TILE = 256
LANES = 128
HD = 64
GRID_W = 64
EPS = 1e-6
ROPE_BASE = 10000.0
NEG = -1e30

N_EXP = 32
TOP_K = 4
SWIGLU_LIMIT = 7.0
SWIGLU_ALPHA = 1.702
MOE_BLK = 256

MLA_Q_LORA = 192
MLA_KV_LORA = 128
MLA_NOPE = 64
MLA_ROPE = 32
NAT_KR = 8
NAT_KC = 16

_IN_SIZES = (256, 128, 128, 256, 256, 256, 512, 256, 256, 16, 192, 128, 32)
_IN_OFF = tuple(sum(_IN_SIZES[:i]) for i in range(len(_IN_SIZES) + 1))

_C_QA, _C_QAR, _C_KA, _C_KAR, _C_VA = 0, 256, 512, 640, 768
_C_QB, _C_KB, _C_VB = 896, 1152, 1408
_C_QK, _C_VC, _C_OC, _C_GI, _C_GF = 1664, 2176, 2432, 2688, 2816
_C_CQ, _C_CKV, _C_KR, _C_KRR = 2944, 3200, 3328, 3456
_NC = 3584

_VMEM_LIMIT = 56 * 1024 * 1024


def _cparams(sem):
    return pltpu.CompilerParams(dimension_semantics=sem, vmem_limit_bytes=_VMEM_LIMIT)


def _dot(a, b):
    return jnp.dot(a, b, preferred_element_type=F32)


def _dot_nt(a, b):
    return lax.dot_general(a, b, (((1,), (1,)), ((), ())), preferred_element_type=F32)


def _split3(x):
    hi = x.astype(MXU)
    r1 = x - hi.astype(F32)
    mid = r1.astype(MXU)
    lo = (r1 - mid.astype(F32)).astype(MXU)
    return hi, mid, lo


def _head_cols(w, order, swap):
    d = w.shape[0]
    nh = w.shape[1] // HD
    w = jnp.stack([w.reshape(d, nh, HD)[:, i, :] for i in order], axis=1)
    if swap:
        w = jnp.concatenate([w[..., HD // 2:], w[..., :HD // 2]], axis=-1)
    return w.reshape(d, len(order) * HD)


def _pack_layer(w_in, gate_bias, w_uq, w_ukv, g_q, g_kv, w_out, conv_w, rpb):
    d = w_in.shape[0]
    o = _IN_OFF
    col = lambda i: w_in[:, o[i]:o[i + 1]]
    z = lambda n: jnp.zeros((d, n), F32)
    qa, ka, va, qb, kb, vb, qk, vc, oc, g, cq, ckv, kr = [col(i) for i in range(13)]
    a_ord = (0, 2, 1, 3)
    sc = HD ** -0.5
    gI = jnp.concatenate([g[:, 0:4], g[:, 8:12], z(120)], axis=1)
    gF = jnp.concatenate([g[:, 4:8], g[:, 12:16], z(120)], axis=1)
    hr = MLA_ROPE // 2
    kr_g = jnp.concatenate([z(64), kr, z(32)], axis=1)
    krr_g = jnp.concatenate([z(64), kr[:, hr:], kr[:, :hr], z(32)], axis=1)
    w1 = jnp.concatenate([
        _head_cols(qa, a_ord, False) * sc, _head_cols(qa, a_ord, True) * sc,
        ka, _head_cols(ka, (0, 1), True), va,
        qb * sc, kb, vb,
        qk, vc, oc, gI, gF,
        cq, z(64), ckv, kr_g, krr_g], axis=1)
    assert w1.shape[1] == _NC
    gb = gate_bias.astype(F32)
    gbias = jnp.stack([jnp.concatenate([gb[0], gb[2], jnp.zeros((120,), F32)]),
                       jnp.concatenate([gb[1], gb[3], jnp.zeros((120,), F32)])])
    dq = MLA_NOPE + MLA_ROPE
    sd = dq ** -0.5
    qh, qrh = [], []
    for h in range(4):
        nope = w_uq[:, h * dq:h * dq + MLA_NOPE]
        rope = w_uq[:, h * dq + MLA_NOPE:(h + 1) * dq]
        zq = lambda n: jnp.zeros((MLA_Q_LORA, n), F32)
        qh.append(jnp.concatenate([nope, rope, zq(32)], axis=1))
        qrh.append(jnp.concatenate([zq(64), rope[:, hr:], rope[:, :hr], zq(32)], axis=1))
    wuq = jnp.concatenate(qh + qrh, axis=1) * sd
    wuq = jnp.concatenate([wuq, jnp.zeros((64, 1024), F32)], axis=0)
    kh = []
    for h in range(4):
        kn = w_ukv[:, h * 128:h * 128 + MLA_NOPE]
        kh.append(jnp.concatenate([kn, jnp.zeros((MLA_KV_LORA, 64), F32)], axis=1))
    vh = [w_ukv[:, h * 128 + MLA_NOPE:(h + 1) * 128] for h in range(4)]
    wukv = jnp.concatenate(kh + vh, axis=1)
    gq = jnp.concatenate([g_q.astype(F32), jnp.zeros((64,), F32)]).reshape(1, 256)
    gkv = g_kv.astype(F32).reshape(1, 128)
    wo = jnp.concatenate([w_out[i * HD:(i + 1) * HD] for i in a_ord] + [w_out[256:]], axis=0)
    return dict(w1=w1.astype(MXU), gbias=gbias, wuq=wuq.astype(MXU), wukv=wukv.astype(MXU),
                gq=gq, gkv=gkv, wo=wo.astype(MXU), conv=conv_w.astype(F32), rpb=rpb.astype(F32))


def _rope_tables(s_len, ctx_len):
    t = jnp.arange(s_len, dtype=jnp.int32)
    rows = (t // GRID_W).astype(F32)[:, None]
    cols = (t % GRID_W).astype(F32)[:, None]

    def ang(rot_dim):
        d_ax = rot_dim // 2
        inv = ROPE_BASE ** (-jnp.arange(0, d_ax, 2, dtype=F32) / d_ax)
        return jnp.concatenate([rows * inv, cols * inv], axis=-1)

    aa = ang(HD)
    ca, sa = jnp.cos(aa), jnp.sin(aa)
    cos_a = jnp.tile(jnp.concatenate([ca, ca], axis=-1), (1, 2))
    sin_a = jnp.tile(jnp.concatenate([-sa, sa], axis=-1), (1, 2))
    ad = ang(MLA_ROPE)
    cd, sd = jnp.cos(ad), jnp.sin(ad)
    one = lambda n: jnp.ones((s_len, n), F32)
    zero = lambda n: jnp.zeros((s_len, n), F32)
    cos_d = jnp.concatenate([one(64), cd, cd, one(32)], axis=-1)
    sin_d = jnp.concatenate([zero(64), -sd, sd, zero(32)], axis=-1)
    pad = lambda a, v: jnp.concatenate([jnp.full((ctx_len, LANES), v, F32), a], axis=0)
    return pad(cos_a, 1.0), pad(sin_a, 0.0), pad(cos_d, 1.0), pad(sin_d, 0.0)


def _mod_kernel(c_ref, w_ref, b_ref, o_ref):
    c = c_ref[...]
    s = c * jax.nn.sigmoid(c)
    w = w_ref[0]
    acc = jnp.zeros(o_ref.shape[1:], F32)
    sp = _split3(s)
    wp = _split3(w)
    for i, j in ((0, 0), (0, 1), (1, 0), (1, 1), (0, 2), (2, 0)):
        acc = acc + _dot(sp[i], wp[j])
    o_ref[0] = acc + b_ref[0]


def _mod_call(c_all, w_ada, b_ada):
    depth, d, n = w_ada.shape
    tn = 768
    return pl.pallas_call(
        _mod_kernel,
        grid=(depth, n // tn),
        in_specs=[pl.BlockSpec((8, d), lambda l, j: (0, 0)),
                  pl.BlockSpec((1, d, tn), lambda l, j: (l, 0, j)),
                  pl.BlockSpec((1, 1, tn), lambda l, j: (l, 0, j))],
        out_specs=pl.BlockSpec((1, 8, tn), lambda l, j: (l, 0, j)),
        out_shape=jax.ShapeDtypeStruct((depth, 8, n), F32),
        compiler_params=_cparams(("parallel", "parallel")),
    )(c_all, w_ada, b_ada.reshape(depth, 1, n))


def _rms(x, n=None):
    n = x.shape[-1] if n is None else n
    return x * lax.rsqrt(jnp.sum(x * x, axis=-1, keepdims=True) * (1.0 / n) + EPS)


def _proj_kernel(x_ref, mod_ref, g1_ref, w1_ref, gb_ref, wuq_ref, wukv_ref, gq_ref, gkv_ref,
                 ca_ref, sa_ref, cd_ref, sd_ref,
                 qa_o, ka_o, va_o, qb_o, kb_o, vb_o, qk_o, vc_o, oc_o, gi_o, gf_o, qd_o, kd_o, vd_o):
    d = x_ref.shape[-1]
    x = x_ref[...]
    mod = mod_ref[0]
    h = _rms(x) * g1_ref[...]
    h = (h * (1.0 + mod[:, d:2 * d]) + mod[:, 0:d]).astype(MXU)
    proj = lambda a, n: _dot(h, w1_ref[:, a:a + n])
    ca, sa, cd, sd = ca_ref[...], sa_ref[...], cd_ref[...], sd_ref[...]

    q, qr = proj(_C_QA, 256), proj(_C_QAR, 256)
    for g in range(2):
        sl = slice(g * LANES, (g + 1) * LANES)
        qa_o[:, sl] = (q[:, sl] * ca + qr[:, sl] * sa).astype(qa_o.dtype)
    ka_o[...] = (proj(_C_KA, 128) * ca + proj(_C_KAR, 128) * sa).astype(ka_o.dtype)
    va_o[...] = proj(_C_VA, 128).astype(va_o.dtype)
    qb_o[...] = proj(_C_QB, 256).astype(qb_o.dtype)
    kb_o[...] = proj(_C_KB, 256).astype(kb_o.dtype)
    vb_o[...] = proj(_C_VB, 256).astype(vb_o.dtype)
    qk_o[...] = proj(_C_QK, 512).astype(qk_o.dtype)
    vc_o[...] = proj(_C_VC, 256).astype(vc_o.dtype)
    oc_o[...] = proj(_C_OC, 256).astype(oc_o.dtype)
    gb = gb_ref[...]
    gi_o[...] = proj(_C_GI, 128) + gb[0:1, :]
    gf_o[...] = proj(_C_GF, 128) + gb[1:2, :]
    cq = proj(_C_CQ, 256)
    cqn = (_rms(cq, MLA_Q_LORA) * gq_ref[...]).astype(MXU)
    q2 = _dot(cqn, wuq_ref[...])
    ckv = proj(_C_CKV, 128)
    ckvn = (_rms(ckv) * gkv_ref[...]).astype(MXU)
    kv2 = _dot(ckvn, wukv_ref[...])
    krope = proj(_C_KR, 128) * cd + proj(_C_KRR, 128) * sd
    for hh in range(4):
        sl = slice(hh * LANES, (hh + 1) * LANES)
        sr = slice(512 + hh * LANES, 512 + (hh + 1) * LANES)
        qd_o[:, sl] = (q2[:, sl] * cd + q2[:, sr] * sd).astype(qd_o.dtype)
        kd_o[:, sl] = (kv2[:, sl] + krope).astype(kd_o.dtype)
    vd_o[...] = kv2[:, 512:768].astype(vd_o.dtype)


def _tile_maps(n_tiles_b, n_batch):
    tok = lambda i: (i, 0)
    pos = lambda i: (i % n_tiles_b, 0)
    modr = lambda i: (jnp.where(i % n_tiles_b == 0, n_batch, i // n_tiles_b), 0, 0)
    return tok, pos, modr


def _proj_call(x_all, mod3, g1, pk, tabs, n_batch):
    t_all, d = x_all.shape
    n_tiles = t_all // TILE
    ntb = n_tiles // n_batch
    tok, pos, modr = _tile_maps(ntb, n_batch)
    const = lambda i: (0, 0)
    full = lambda a: pl.BlockSpec(a.shape, const)
    widths = (256, 128, 128, 256, 256, 256, 512, 256, 256, 128, 128, 512, 512, 256)
    dts = (MXU,) * 9 + (F32, F32) + (MXU,) * 3
    tab = pl.BlockSpec((TILE, LANES), pos)
    return pl.pallas_call(
        _proj_kernel,
        grid=(n_tiles,),
        in_specs=[pl.BlockSpec((TILE, d), tok), pl.BlockSpec((1, 1, mod3.shape[-1]), modr), full(g1),
                  full(pk["w1"]), full(pk["gbias"]), full(pk["wuq"]), full(pk["wukv"]),
                  full(pk["gq"]), full(pk["gkv"]), tab, tab, tab, tab],
        out_specs=[pl.BlockSpec((TILE, w), tok) for w in widths],
        out_shape=[jax.ShapeDtypeStruct((t_all, w), dt) for w, dt in zip(widths, dts)],
        compiler_params=_cparams(("parallel",)),
    )(x_all, mod3, g1, pk["w1"], pk["gbias"], pk["wuq"], pk["wukv"], pk["gq"], pk["gkv"], *tabs)


_A_ORD = (0, 2, 1, 3)
A_BLK = 128
A_WIN = 128


def _mixa_kernel(sink_ref, q_ref, kp_ref, kc_ref, kn_ref, kx_ref, vp_ref, vc_ref, vn_ref, vx_ref, o_ref,
                 *, s_len, ctx_blocks):
    n = pl.program_id(1)
    is_lat = n >= ctx_blocks
    nb = 3 * A_BLK
    lane = lax.broadcasted_iota(jnp.int32, (A_BLK, LANES), 1)
    lo = lane < HD
    qi = lax.broadcasted_iota(jnp.int32, (2 * A_BLK, nb), 0) & (A_BLK - 1)
    kj = lax.broadcasted_iota(jnp.int32, (2 * A_BLK, nb), 1)
    qpos = (n - ctx_blocks) * A_BLK + qi
    kpos = (n - ctx_blocks - 1) * A_BLK + kj
    ok = is_lat & (kpos >= 0) & (kpos < s_len) & (jnp.abs(qpos - kpos) <= A_WIN)
    row = lax.broadcasted_iota(jnp.int32, (2 * A_BLK, 1), 0)
    k_all = jnp.concatenate([kp_ref[...], kc_ref[...], kn_ref[...], kx_ref[...]], axis=0)
    v_all = jnp.concatenate([vp_ref[...], vc_ref[...], vn_ref[...], vx_ref[...]], axis=0)
    q = q_ref[...]
    zero = jnp.zeros((A_BLK, LANES), q.dtype)
    for g in range(2):
        qg = q[:, g * LANES:(g + 1) * LANES]
        q2 = jnp.concatenate([jnp.where(lo, qg, zero), jnp.where(lo, zero, qg)], axis=0)
        s = _dot_nt(q2, k_all)
        s = jnp.concatenate([jnp.where(ok, s[:, :nb], NEG), s[:, nb:]], axis=1)
        sink = jnp.where(row < A_BLK, sink_ref[_A_ORD[2 * g]], sink_ref[_A_ORD[2 * g + 1]])
        m = jnp.maximum(jnp.max(s, axis=-1, keepdims=True), sink)
        p = jnp.exp(s - m)
        den = jnp.sum(p, axis=-1, keepdims=True) + jnp.exp(sink - m)
        pv = _dot(p.astype(MXU), v_all) / den
        o_ref[:, g * LANES:(g + 1) * LANES] = jnp.where(lo, pv[:A_BLK], pv[A_BLK:]).astype(o_ref.dtype)


def _mixa_call(qa, ka, va, sink, n_batch, s_len):
    t_all = qa.shape[0]
    nq = t_all // n_batch // A_BLK
    cb = TILE // A_BLK
    qmap = lambda b, n: (b * nq + n, 0)
    pmap = lambda b, n: (b * nq + jnp.clip(n - 1, cb, nq - 1), 0)
    nmap = lambda b, n: (b * nq + jnp.clip(n + 1, cb, nq - 1), 0)
    xmap = lambda b, n: (b * (nq // cb), 0)
    kv = lambda m: pl.BlockSpec((A_BLK, LANES), m)
    kvx = pl.BlockSpec((TILE, LANES), xmap)
    return pl.pallas_call(
        functools.partial(_mixa_kernel, s_len=s_len, ctx_blocks=cb),
        grid=(n_batch, nq),
        in_specs=[pl.BlockSpec(memory_space=pltpu.SMEM), pl.BlockSpec((A_BLK, 256), qmap),
                  kv(pmap), kv(qmap), kv(nmap), kvx, kv(pmap), kv(qmap), kv(nmap), kvx],
        out_specs=pl.BlockSpec((A_BLK, 256), qmap),
        out_shape=jax.ShapeDtypeStruct((t_all, 256), MXU),
        compiler_params=_cparams(("parallel", "parallel")),
    )(sink.astype(F32), qa, ka, ka, ka, ka, va, va, va, va)


def _nat_bias_table(rpb):
    qc = jnp.arange(GRID_W)[:, None]
    kc = jnp.arange(GRID_W)[None, :]
    cs = jnp.clip(qc - NAT_KC // 2, 0, GRID_W - NAT_KC)
    ok = (kc >= cs) & (kc < cs + NAT_KC)
    dc = jnp.clip(kc - qc + NAT_KC - 1, 0, 2 * NAT_KC - 2)
    t = jnp.where(ok[None, None], rpb[:, :, dc], NEG)
    pair = lambda h: jnp.concatenate([t[h, :-1], t[h, 1:]], axis=-1)
    return jnp.stack([jnp.concatenate([pair(2 * g), pair(2 * g + 1)], axis=1) for g in range(2)])


def _mixb_kernel(q_ref, k_ref, v_ref, tb_ref, o_ref, *, rows):
    j = pl.program_id(1)
    is_ctx = j == 0
    nloc = NAT_KR * GRID_W
    lane = lax.broadcasted_iota(jnp.int32, (GRID_W, LANES), 1)
    lo = lane < HD
    zero = jnp.zeros((GRID_W, LANES), q_ref.dtype)
    for i in range(TILE // GRID_W):
        r = jnp.maximum((j - 1) * (TILE // GRID_W) + i, 0)
        r0 = jnp.clip(r - NAT_KR // 2, 0, rows - NAT_KR)
        kstart = pl.multiple_of(TILE + r0 * GRID_W, GRID_W)
        dr0 = r0 - r + NAT_KR - 1
        for g in range(2):
            sl = slice(g * LANES, (g + 1) * LANES)
            qg = q_ref[i * GRID_W:(i + 1) * GRID_W, sl]
            q2 = jnp.concatenate([jnp.where(lo, qg, zero), jnp.where(lo, zero, qg)], axis=0)
            s_loc = _dot_nt(q2, k_ref[pl.ds(kstart, nloc), sl])
            bias = jnp.concatenate([tb_ref[g, dr0 + 2 * t] for t in range(NAT_KR // 2)], axis=1)
            s_loc = jnp.where(is_ctx, NEG, s_loc + bias)
            s_ctx = _dot_nt(q2, k_ref[0:TILE, sl])
            m = jnp.maximum(jnp.max(s_loc, axis=-1, keepdims=True), jnp.max(s_ctx, axis=-1, keepdims=True))
            p_loc = jnp.exp(s_loc - m)
            p_ctx = jnp.exp(s_ctx - m)
            den = jnp.sum(p_loc, axis=-1, keepdims=True) + jnp.sum(p_ctx, axis=-1, keepdims=True)
            pv = _dot(p_loc.astype(MXU), v_ref[pl.ds(kstart, nloc), sl]) + _dot(p_ctx.astype(MXU), v_ref[0:TILE, sl])
            pv = pv / den
            o_ref[i * GRID_W:(i + 1) * GRID_W, sl] = jnp.where(lo, pv[:GRID_W], pv[GRID_W:]).astype(o_ref.dtype)


def _mixb_call(qb, kb, vb, tb, n_batch, s_len):
    t_all = qb.shape[0]
    npb = t_all // n_batch
    ntb = npb // TILE
    qmap = lambda b, j: (b * ntb + j, 0)
    kvmap = lambda b, j: (b, 0, 0)
    kvspec = pl.BlockSpec((None, npb, 256), kvmap)
    return pl.pallas_call(
        functools.partial(_mixb_kernel, rows=s_len // GRID_W),
        grid=(n_batch, ntb),
        in_specs=[pl.BlockSpec((TILE, 256), qmap), kvspec, kvspec,
                  pl.BlockSpec(tb.shape, lambda b, j: (0, 0, 0, 0))],
        out_specs=pl.BlockSpec((TILE, 256), qmap),
        out_shape=jax.ShapeDtypeStruct((t_all, 256), MXU),
        compiler_params=_cparams(("parallel", "arbitrary")),
    )(qb, kb.reshape(n_batch, npb, 256), vb.reshape(n_batch, npb, 256), tb)


def _mixd_kernel(q_ref, k_ref, v_ref, o_ref, m_sc, l_sc, acc_sc):
    j = pl.program_id(1)
    nk = jnp.where(j == 0, 1, pl.num_programs(1))
    m_sc[...] = jnp.full(m_sc.shape, NEG, F32)
    l_sc[...] = jnp.zeros(l_sc.shape, F32)
    acc_sc[...] = jnp.zeros(acc_sc.shape, F32)
    lo = lax.broadcasted_iota(jnp.int32, (TILE, LANES), 1) < HD

    def body(c, carry):
        ks = pl.multiple_of(c * TILE, TILE)
        for p in range(2):
            vp = v_ref[pl.ds(ks, TILE), p * LANES:(p + 1) * LANES]
            alphas, pvs = [], []
            for hh in range(2):
                h = 2 * p + hh
                sl = slice(h * LANES, (h + 1) * LANES)
                s = _dot_nt(q_ref[:, sl], k_ref[pl.ds(ks, TILE), sl])
                m_old = m_sc[h]
                m_new = jnp.maximum(m_old, jnp.max(s, axis=-1, keepdims=True))
                alpha = jnp.exp(m_old - m_new)
                pe = jnp.exp(s - m_new)
                l_sc[h] = alpha * l_sc[h] + jnp.sum(pe, axis=-1, keepdims=True)
                m_sc[h] = m_new
                alphas.append(alpha)
                pvs.append(_dot(pe.astype(MXU), vp))
            acc_sc[p] = acc_sc[p] * jnp.where(lo, alphas[0], alphas[1]) + jnp.where(lo, pvs[0], pvs[1])
        return carry

    lax.fori_loop(0, nk, body, 0)
    for p in range(2):
        den = jnp.where(lo, l_sc[2 * p], l_sc[2 * p + 1])
        o_ref[:, p * LANES:(p + 1) * LANES] = (acc_sc[p] / den).astype(o_ref.dtype)


def _mixd_call(qd, kd, vd, n_batch):
    t_all = qd.shape[0]
    npb = t_all // n_batch
    ntb = npb // TILE
    qmap = lambda b, j: (b * ntb + j, 0)
    kvmap = lambda b, j: (b, 0, 0)
    return pl.pallas_call(
        _mixd_kernel,
        grid=(n_batch, ntb),
        in_specs=[pl.BlockSpec((TILE, 512), qmap), pl.BlockSpec((None, npb, 512), kvmap),
                  pl.BlockSpec((None, npb, 256), kvmap)],
        out_specs=pl.BlockSpec((TILE, 256), qmap),
        out_shape=jax.ShapeDtypeStruct((t_all, 256), MXU),
        scratch_shapes=[pltpu.VMEM((4, TILE, 1), F32), pltpu.VMEM((4, TILE, 1), F32),
                        pltpu.VMEM((2, TILE, LANES), F32)],
        compiler_params=_cparams(("parallel", "arbitrary")),
    )(qd, kd.reshape(n_batch, npb, 512), vd.reshape(n_batch, npb, 256))


C_L = 128
HALO = 16


def _log_sigmoid(x):
    return jnp.minimum(x, 0.0) - jnp.log1p(jnp.exp(-jnp.abs(x)))


def _mlstm_chunk(d, j, i, cb, nch, qk_ref, hp_ref, hn_ref, v_ref, gi_ref, gf_ref, conv_ref, o_ref,
                 c_sc, n_sc, m_sc):
    row = lax.broadcasted_iota(jnp.int32, (C_L, 1), 0)
    rr = lax.broadcasted_iota(jnp.int32, (C_L, C_L), 0)
    cc = lax.broadcasted_iota(jnp.int32, (C_L, C_L), 1)
    lo = cc < HD
    causal = (cc <= rr) if d == 0 else (cc >= rr)
    last = C_L - 1 if d == 0 else 0
    blockdiag = (rr < HD) == (cc < HD)

    prev_ok = (j != 0) & (j != cb)
    next_ok = (j != cb - 1) & (j != nch - 1)
    x = qk_ref[...].astype(F32)
    prow = jnp.where(prev_ok, hp_ref[HALO - 1:HALO, :].astype(F32), 0.0)
    nrow = jnp.where(next_ok, hn_ref[0:1, :].astype(F32), 0.0)
    xm1 = jnp.where(row == 0, prow, pltpu.roll(x, 1, 0))
    xp1 = jnp.where(row == C_L - 1, nrow, pltpu.roll(x, C_L - 1, 0))
    w = conv_ref[...]
    u = xm1 * w[0:1, :] + x * w[1:2, :] + xp1 * w[2:3, :]
    a = u * jax.nn.sigmoid(u)
    q_all = a[:, 0:256].astype(MXU)
    k_all = a[:, 256:512] * (HD ** -0.5)

    f = _log_sigmoid(gf_ref[...])
    tri = causal.astype(MXU)
    bc = sum(_dot(tri, part) for part in _split3(f))
    g = gi_ref[...] - bc
    gt = g.T

    for p in range(2):
        sl = slice(p * LANES, (p + 1) * LANES)
        qp = q_all[:, sl]
        kp32 = k_all[:, sl]
        kp = kp32.astype(MXU)
        vp = v_ref[:, sl]
        cmat = c_sc[d, p]
        nmat = n_sc[d, p]
        c16, n16 = cmat.astype(MXU), nmat.astype(MXU)
        zero = jnp.zeros_like(qp)
        houts, wcols, decays, scls = [], [], [], []
        for hh in range(2):
            ln = d * 4 + 2 * p + hh
            qh = jnp.where(lo, qp, zero) if hh == 0 else jnp.where(lo, zero, qp)
            bcol = bc[:, ln:ln + 1]
            logd = jnp.where(causal, bcol + gt[ln:ln + 1, :], NEG)
            mloc = jnp.max(logd, axis=-1, keepdims=True)
            m_prev = m_sc[ln:ln + 1, 0:1]
            m_t = jnp.maximum(bcol + m_prev, mloc)
            sw = _dot_nt(qh, kp) * jnp.exp(logd - m_t)
            inter = jnp.exp(bcol + m_prev - m_t)
            num = _dot(sw.astype(MXU), vp) + inter * _dot(qh, c16)
            den = jnp.sum(sw, axis=-1, keepdims=True) + inter * _dot(qh, n16)
            houts.append(num / jnp.maximum(jnp.abs(den), jnp.exp(-m_t)))
            a_end = mloc[last:last + 1, :]
            b_end = bcol[last:last + 1, :]
            m_new = jnp.maximum(b_end + m_prev, a_end)
            decays.append(jnp.exp(b_end + m_prev - m_new))
            scls.append(jnp.exp(a_end - m_new))
            wcols.append(jnp.exp(g[:, ln:ln + 1] + (b_end - a_end)))
            m_sc[ln:ln + 1, :] = jnp.broadcast_to(m_new, (1, LANES))
        o_ref[:, sl] = jnp.where(lo, houts[0], houts[1])
        kwt = (kp32 * jnp.where(lo, wcols[0], wcols[1])).T.astype(MXU)
        kv = jnp.where(blockdiag, _dot(kwt, vp), 0.0)
        ksum = _dot(kwt, jnp.ones((C_L, LANES), MXU))
        dec = jnp.where(row < HD, decays[0], decays[1])
        scl = jnp.where(row < HD, scls[0], scls[1])
        c_sc[d, p] = dec * cmat + scl * kv
        n_sc[d, p] = dec * nmat + scl * ksum


def _mixc_kernel(qkf, hpf, hnf, vf, gif, gff, qkb, hpb, hnb, vb, gib, gfb, conv_ref, of_ref, ob_ref,
                 c_sc, n_sc, m_sc, *, cb, nch):
    i = pl.program_id(1)

    @pl.when(i == 0)
    def _():
        c_sc[...] = jnp.zeros(c_sc.shape, F32)
        n_sc[...] = jnp.zeros(n_sc.shape, F32)
        m_sc[...] = jnp.zeros(m_sc.shape, F32)

    jb = jnp.where(i < cb, cb - 1 - i, nch + cb - 1 - i)
    _mlstm_chunk(0, i, i, cb, nch, qkf, hpf, hnf, vf, gif, gff, conv_ref, of_ref, c_sc, n_sc, m_sc)
    _mlstm_chunk(1, jb, i, cb, nch, qkb, hpb, hnb, vb, gib, gfb, conv_ref, ob_ref, c_sc, n_sc, m_sc)


def _mixc_call(qk, vc, gi, gf, conv_w, n_batch):
    t_all = qk.shape[0]
    nch = t_all // n_batch // C_L
    cb = TILE // C_L
    hb = C_L // HALO
    n_halo = t_all // HALO
    jf = lambda i: i
    jb = lambda i: jnp.where(i < cb, cb - 1 - i, nch + cb - 1 - i)

    def specs(jmap):
        cur = lambda b, i: (b * nch + jmap(i), 0)
        prv = lambda b, i: (jnp.maximum((b * nch + jmap(i)) * hb - 1, 0), 0)
        nxt = lambda b, i: (jnp.minimum((b * nch + jmap(i) + 1) * hb, n_halo - 1), 0)
        return cur, [pl.BlockSpec((C_L, 512), cur), pl.BlockSpec((HALO, 512), prv), pl.BlockSpec((HALO, 512), nxt),
                     pl.BlockSpec((C_L, 256), cur), pl.BlockSpec((C_L, LANES), cur), pl.BlockSpec((C_L, LANES), cur)]

    cur_f, in_f = specs(jf)
    cur_b, in_b = specs(jb)
    args = (qk, qk, qk, vc, gi, gf)
    return pl.pallas_call(
        functools.partial(_mixc_kernel, cb=cb, nch=nch),
        grid=(n_batch, nch),
        in_specs=in_f + in_b + [pl.BlockSpec(conv_w.shape, lambda b, i: (0, 0))],
        out_specs=[pl.BlockSpec((C_L, 256), cur_f), pl.BlockSpec((C_L, 256), cur_b)],
        out_shape=[jax.ShapeDtypeStruct((t_all, 256), F32)] * 2,
        scratch_shapes=[pltpu.VMEM((2, 2, LANES, LANES), F32), pltpu.VMEM((2, 2, LANES, LANES), F32),
                        pltpu.VMEM((8, LANES), F32)],
        compiler_params=_cparams(("parallel", "arbitrary")),
    )(*args, *args, conv_w)


def _out_kernel(x_ref, mod_ref, ya_ref, yb_ref, hf_ref, hb_ref, oc_ref, yd_ref, wo_ref, g2_ref, wr_ref, br_ref,
                x1_o, h2_o, idx_o, rank_o, gate_o, cnt_o, carry_sc):
    d = x_ref.shape[-1]
    i = pl.program_id(0)

    @pl.when(i == 0)
    def _():
        carry_sc[...] = jnp.zeros(carry_sc.shape, F32)

    mod = mod_ref[0]
    ym = (jax.nn.sigmoid(oc_ref[...].astype(F32)) * (hf_ref[...] + hb_ref[...])).astype(MXU)
    acc = (_dot(ya_ref[...], wo_ref[0:256, :]) + _dot(yb_ref[...], wo_ref[256:512, :])
           + _dot(ym, wo_ref[512:768, :]) + _dot(yd_ref[...], wo_ref[768:1024, :]))
    x1 = x_ref[...] + mod[:, 2 * d:3 * d] * acc
    x1_o[...] = x1
    h2 = _rms(x1) * g2_ref[...]
    h2 = h2 * (1.0 + mod[:, 4 * d:5 * d]) + mod[:, 3 * d:4 * d]
    h2_o[...] = h2

    hp = _split3(h2)
    wp = _split3(wr_ref[...])
    logits = br_ref[...]
    for a, b in ((0, 0), (0, 1), (1, 0), (1, 1), (0, 2), (2, 0)):
        logits = logits + _dot(hp[a], wp[b])
    lane = lax.broadcasted_iota(jnp.int32, (TILE, LANES), 1)
    vals, idxs = [], []
    cur = logits
    for _ in range(TOP_K):
        mk = jnp.max(cur, axis=-1, keepdims=True)
        ik = jnp.min(jnp.where(cur == mk, lane, LANES), axis=-1, keepdims=True)
        cur = jnp.where(lane == ik, NEG, cur)
        vals.append(mk)
        idxs.append(ik)
    es = [jnp.exp(v - vals[0]) for v in vals]
    esum = es[0] + es[1] + es[2] + es[3]
    hot = [(lane == ik) for ik in idxs]
    multi = sum(h.astype(F32) for h in hot)
    rr = lax.broadcasted_iota(jnp.int32, (TILE, TILE), 0)
    cc = lax.broadcasted_iota(jnp.int32, (TILE, TILE), 1)
    before = _dot((cc < rr).astype(MXU), multi.astype(MXU)) + carry_sc[...]
    carry_sc[...] = carry_sc[...] + jnp.sum(multi, axis=0, keepdims=True)
    idx_t = jnp.zeros((TILE, LANES), jnp.int32)
    rank_t = jnp.zeros((TILE, LANES), jnp.int32)
    gate_t = jnp.zeros((TILE, LANES), F32)
    for k in range(TOP_K):
        rk = jnp.sum(jnp.where(hot[k], before, 0.0), axis=-1, keepdims=True).astype(jnp.int32)
        idx_t = jnp.where(lane == k, idxs[k], idx_t)
        rank_t = jnp.where(lane == k, rk, rank_t)
        gate_t = jnp.where(lane == k, es[k] / esum, gate_t)
    idx_o[...] = idx_t
    rank_o[...] = rank_t
    gate_o[...] = gate_t
    cnt_o[...] = jnp.broadcast_to(carry_sc[...], cnt_o.shape)


def _out_call(x_all, mod3, ya, yb, hf, hb, oc, yd, wo, g2, wr, br, n_batch):
    t_all, d = x_all.shape
    n_tiles = t_all // TILE
    tok, _, modr = _tile_maps(n_tiles // n_batch, n_batch)
    const = lambda i: (0, 0)
    full = lambda a: pl.BlockSpec(a.shape, const)
    t256 = pl.BlockSpec((TILE, 256), tok)
    td = pl.BlockSpec((TILE, d), tok)
    tl = pl.BlockSpec((TILE, LANES), tok)
    return pl.pallas_call(
        _out_kernel,
        grid=(n_tiles,),
        in_specs=[td, pl.BlockSpec((1, 1, mod3.shape[-1]), modr), t256, t256, t256, t256, t256, t256,
                  full(wo), full(g2), full(wr), full(br)],
        out_specs=[td, td, tl, tl, tl, pl.BlockSpec((8, LANES), const)],
        out_shape=[jax.ShapeDtypeStruct((t_all, d), F32), jax.ShapeDtypeStruct((t_all, d), F32),
                   jax.ShapeDtypeStruct((t_all, LANES), jnp.int32), jax.ShapeDtypeStruct((t_all, LANES), jnp.int32),
                   jax.ShapeDtypeStruct((t_all, LANES), F32), jax.ShapeDtypeStruct((8, LANES), F32)],
        scratch_shapes=[pltpu.VMEM((1, LANES), F32)],
        compiler_params=_cparams(("arbitrary",)),
    )(x_all, mod3, ya, yb, hf, hb, oc, yd, wo, g2, wr, br)


def _route_plan(idx, rank, counts, t_all):
    cnt = counts[0, :N_EXP].astype(jnp.int32)
    padded = (cnt + MOE_BLK - 1) // MOE_BLK * MOE_BLK
    pad_end = jnp.cumsum(padded)
    pad_start = pad_end - padded
    e = idx[:, :TOP_K]
    start_of = jnp.sum(jnp.where(e[..., None] == jnp.arange(N_EXP), pad_start, 0), axis=-1)
    dest = (start_of + rank[:, :TOP_K]).astype(jnp.int32)
    n_blk = (t_all * TOP_K + N_EXP * (MOE_BLK - 1) + MOE_BLK - 1) // MOE_BLK
    blk_exp = jnp.minimum(jnp.searchsorted(pad_end, jnp.arange(n_blk, dtype=jnp.int32) * MOE_BLK, side="right"),
                          N_EXP - 1).astype(jnp.int32)
    n_used = (pad_end[-1] // MOE_BLK).astype(jnp.int32).reshape(1)
    return dest.reshape(t_all // TILE, 1, TILE * TOP_K), blk_exp, n_used, n_blk


def _dispatch_kernel(dest_ref, h2_ref, xs_in, xs_o, sem):
    del xs_in
    n = TILE * TOP_K

    def row_copy(a, dst_row):
        return pltpu.make_async_copy(h2_ref.at[pl.ds(a // TOP_K, 1), :], xs_o.at[pl.ds(dst_row, 1), :], sem)

    def start(a, c):
        row_copy(a, dest_ref[0, 0, a]).start()
        return c

    def wait(a, c):
        row_copy(a, 0).wait()
        return c

    lax.fori_loop(0, n, start, 0)
    lax.fori_loop(0, n, wait, 0)


def _dispatch_call(dest3, h2, cap):
    t_all, d = h2.shape
    n_tiles = t_all // TILE
    xs0 = jnp.zeros((cap, d), F32)
    return pl.pallas_call(
        _dispatch_kernel,
        grid=(n_tiles,),
        in_specs=[pl.BlockSpec((1, 1, TILE * TOP_K), lambda i: (i, 0, 0), memory_space=pltpu.SMEM),
                  pl.BlockSpec((TILE, d), lambda i: (i, 0)), pl.BlockSpec(memory_space=pl.ANY)],
        out_specs=pl.BlockSpec(memory_space=pl.ANY),
        out_shape=jax.ShapeDtypeStruct((cap, d), F32),
        scratch_shapes=[pltpu.SemaphoreType.DMA(())],
        input_output_aliases={2: 0},
        compiler_params=_cparams(("arbitrary",)),
    )(dest3, h2, xs0)


def _expert_kernel(be_ref, nu_ref, xs_ref, wg_ref, wu_ref, bg_ref, bu_ref, wd_ref, bd_ref, y_ref):
    n = pl.program_id(0)

    @pl.when(n < nu_ref[0])
    def _():
        xb = xs_ref[...].astype(MXU)
        g = jnp.minimum(_dot(xb, wg_ref[...]) + bg_ref[...], SWIGLU_LIMIT)
        u = jnp.clip(_dot(xb, wu_ref[...]) + bu_ref[...], -SWIGLU_LIMIT, SWIGLU_LIMIT)
        act = (u + 1.0) * g * jax.nn.sigmoid(SWIGLU_ALPHA * g)
        y_ref[...] = _dot(act.astype(MXU), wd_ref[...]) + bd_ref[...]

    @pl.when(n >= nu_ref[0])
    def _():
        y_ref[...] = jnp.zeros(y_ref.shape, F32)


def _expert_call(blk_exp, n_used, xs, wg, wu, bg, bu, wd, bd):
    cap, d = xs.shape
    dff = wg.shape[-1]
    n_blk = cap // MOE_BLK
    emap = lambda n, be, nu: (be[n], 0, 0)
    rows = lambda n, be, nu: (n, 0)
    return pl.pallas_call(
        _expert_kernel,
        grid_spec=pltpu.PrefetchScalarGridSpec(
            num_scalar_prefetch=2,
            grid=(n_blk,),
            in_specs=[pl.BlockSpec((MOE_BLK, d), rows),
                      pl.BlockSpec((None, d, dff), emap), pl.BlockSpec((None, d, dff), emap),
                      pl.BlockSpec((None, 1, dff), emap), pl.BlockSpec((None, 1, dff), emap),
                      pl.BlockSpec((None, dff, d), emap), pl.BlockSpec((None, 1, d), emap)],
            out_specs=pl.BlockSpec((MOE_BLK, d), rows)),
        out_shape=jax.ShapeDtypeStruct((cap, d), F32),
        compiler_params=_cparams(("arbitrary",)),
    )(blk_exp, n_used, xs, wg, wu, bg, bu, wd, bd)


def _combine_kernel(dest_ref, gate_ref, x1_ref, mod_ref, y_hbm, x2_o, buf, sem):
    d = x1_ref.shape[-1]
    n = TILE * TOP_K

    def row_copy(a, src_row):
        return pltpu.make_async_copy(y_hbm.at[pl.ds(src_row, 1), :],
                                     buf.at[a % TOP_K, pl.ds(a // TOP_K, 1), :], sem)

    def start(a, c):
        row_copy(a, dest_ref[0, 0, a]).start()
        return c

    def wait(a, c):
        row_copy(a, 0).wait()
        return c

    lax.fori_loop(0, n, start, 0)
    lax.fori_loop(0, n, wait, 0)
    gate = gate_ref[...]
    f = gate[:, 0:1] * buf[0]
    for k in range(1, TOP_K):
        f = f + gate[:, k:k + 1] * buf[k]
    x2_o[...] = x1_ref[...] + mod_ref[0][:, 5 * d:6 * d] * f


def _combine_call(dest3, gate, x1, mod3, y, n_batch):
    t_all, d = x1.shape
    n_tiles = t_all // TILE
    tok, _, modr = _tile_maps(n_tiles // n_batch, n_batch)
    return pl.pallas_call(
        _combine_kernel,
        grid=(n_tiles,),
        in_specs=[pl.BlockSpec((1, 1, TILE * TOP_K), lambda i: (i, 0, 0), memory_space=pltpu.SMEM),
                  pl.BlockSpec((TILE, LANES), tok), pl.BlockSpec((TILE, d), tok),
                  pl.BlockSpec((1, 1, mod3.shape[-1]), modr), pl.BlockSpec(memory_space=pl.ANY)],
        out_specs=pl.BlockSpec((TILE, d), tok),
        out_shape=jax.ShapeDtypeStruct((t_all, d), F32),
        scratch_shapes=[pltpu.VMEM((TOP_K, TILE, d), F32), pltpu.SemaphoreType.DMA(())],
        compiler_params=_cparams(("arbitrary",)),
    )(dest3, gate, x1, mod3, y)


def _final_kernel(x_ref, g_ref, o_ref):
    o_ref[...] = _rms(x_ref[...]) * g_ref[...]


def _final_call(x_all, g_final, n_batch, s_len):
    t_all, d = x_all.shape
    ntb = t_all // n_batch // TILE
    nlat = s_len // TILE
    return pl.pallas_call(
        _final_kernel,
        grid=(n_batch, nlat),
        in_specs=[pl.BlockSpec((TILE, d), lambda b, j: (b * ntb + 1 + j, 0)), pl.BlockSpec((1, d), lambda b, j: (0, 0))],
        out_specs=pl.BlockSpec((None, TILE, d), lambda b, j: (b, j, 0)),
        out_shape=jax.ShapeDtypeStruct((n_batch, s_len, d), F32),
        compiler_params=_cparams(("parallel", "parallel")),
    )(x_all, g_final.reshape(1, d))


def kernel(x, c, ctx, c_ctx, w_ada, b_ada, g_norm1, g_norm2, w_in, attn_sink, nat_rpb, mlstm_conv, mlstm_gate_bias, mla_g_q, mla_w_uq, mla_g_kv, mla_w_ukv, w_out, w_router, b_router, w_gu, b_gu, w_down, b_down, g_final):
    n_batch, s_len, d = x.shape
    ctx_len = ctx.shape[1]
    assert ctx_len == TILE and s_len % TILE == 0 and s_len // GRID_W >= NAT_KR
    depth = w_ada.shape[0]
    npb = ctx_len + s_len
    c_all = jnp.concatenate([c, c_ctx[None], jnp.zeros((8 - n_batch - 1, d), F32)], axis=0)
    mod_all = _mod_call(c_all, w_ada, b_ada)
    tabs = _rope_tables(s_len, ctx_len)
    x_all = jnp.concatenate([ctx, x], axis=1).reshape(n_batch * npb, d)
    for l in range(depth):
        pk = _pack_layer(w_in[l], mlstm_gate_bias[l], mla_w_uq[l], mla_w_ukv[l], mla_g_q[l], mla_g_kv[l],
                         w_out[l], mlstm_conv[l], nat_rpb[l])
        mod3 = mod_all[l].reshape(8, 1, 6 * d)
        (qa, ka, va, qb, kb, vb, qk, vc, oc, gi, gf, qd, kd, vd) = _proj_call(
            x_all, mod3, g_norm1[l].reshape(1, d), pk, tabs, n_batch)
        ya = _mixa_call(qa, ka, va, attn_sink[l], n_batch, s_len)
        yb = _mixb_call(qb, kb, vb, _nat_bias_table(pk["rpb"]), n_batch, s_len)
        hf, hb = _mixc_call(qk, vc, gi, gf, pk["conv"], n_batch)
        yd = _mixd_call(qd, kd, vd, n_batch)
        x_all = _ffn_layer(x_all, mod3, ya, yb, hf, hb, oc, yd, pk["wo"], g_norm2[l], w_router[l], b_router[l],
                           w_gu[l], b_gu[l], w_down[l], b_down[l], n_batch)
    return _final_call(x_all, g_final, n_batch, s_len)


def _ffn_layer(x_all, mod3, ya, yb, hf, hb, oc, yd, wo, g2, w_router, b_router, w_gu, b_gu, w_down, b_down, n_batch):
    t_all, d = x_all.shape
    wr = jnp.concatenate([w_router.astype(F32), jnp.zeros((d, LANES - N_EXP), F32)], axis=1)
    br = jnp.concatenate([b_router.astype(F32), jnp.full((LANES - N_EXP,), NEG, F32)]).reshape(1, LANES)
    x1, h2, idx, rank, gate, counts = _out_call(x_all, mod3, ya, yb, hf, hb, oc, yd, wo, g2.reshape(1, d),
                                                wr, br, n_batch)
    dest3, blk_exp, n_used, n_blk = _route_plan(idx, rank, counts, t_all)
    xs = _dispatch_call(dest3, h2, n_blk * MOE_BLK)
    wg, wu = w_gu[:, :, 0::2].astype(MXU), w_gu[:, :, 1::2].astype(MXU)
    bg, bu = b_gu[:, None, 0::2].astype(F32), b_gu[:, None, 1::2].astype(F32)
    y = _expert_call(blk_exp, n_used, xs, wg, wu, bg, bu, w_down.astype(MXU), b_down[:, None, :].astype(F32))
    return _combine_call(dest3, gate, x1, mod3, y, n_batch)
```

```python
import functools

import jax
import jax.numpy as jnp
from jax import lax
from jax.experimental import pallas as pl
from jax.experimental.pallas import tpu as pltpu

F32 = jnp.float32
MXU = jnp.bfloat16

TILE = 256
PROJ_ROWS = 768
LANES = 128
HD = 64
GRID_W = 64
EPS = 1e-6
ROPE_BASE = 10000.0
NEG = -1e30
_LOG2E = 1.4426950408889634
D_VROWS = HD + 16

N_EXP = 32
TOP_K = 4
SWIGLU_LIMIT = 7.0
SWIGLU_ALPHA = 1.702
MOE_BLK = 256
RUN_ALIGN = 8

MLA_Q_LORA = 192
MLA_KV_LORA = 128
MLA_NOPE = 64
MLA_ROPE = 32
NAT_KR = 8
NAT_KC = 16

_IN_SIZES = (256, 128, 128, 256, 256, 256, 512, 256, 256, 16, 192, 128, 32)
_IN_OFF = tuple(sum(_IN_SIZES[:i]) for i in range(len(_IN_SIZES) + 1))

_C_QA, _C_QAR, _C_KA, _C_KAR, _C_VA = 0, 256, 512, 640, 768
_C_QB, _C_KB, _C_VB = 896, 1152, 1408
_C_QK, _C_VC, _C_OC, _C_GI, _C_GF = 1664, 2176, 2432, 2688, 2816
_C_CQ, _C_CKV, _C_KR, _C_KRR = 2944, 3200, 3328, 3456
_NC = 3584

_VMEM_LIMIT = 56 * 1024 * 1024


def _cparams(sem):
    return pltpu.CompilerParams(dimension_semantics=sem, vmem_limit_bytes=_VMEM_LIMIT)


def _dot(a, b):
    return jnp.dot(a, b, preferred_element_type=F32)


def _dot_nt(a, b):
    return lax.dot_general(a, b, (((1,), (1,)), ((), ())), preferred_element_type=F32)


def _split3(x):
    hi = x.astype(MXU)
    r1 = x - hi.astype(F32)
    mid = r1.astype(MXU)
    lo = (r1 - mid.astype(F32)).astype(MXU)
    return hi, mid, lo


def _head_cols(w, order, swap):
    d = w.shape[0]
    nh = w.shape[1] // HD
    w = jnp.stack([w.reshape(d, nh, HD)[:, i, :] for i in order], axis=1)
    if swap:
        w = jnp.concatenate([w[..., HD // 2:], w[..., :HD // 2]], axis=-1)
    return w.reshape(d, len(order) * HD)


def _pack_layer(w_in, gate_bias, w_uq, w_ukv, g_q, g_kv, w_out, conv_w, rpb):
    d = w_in.shape[0]
    o = _IN_OFF
    col = lambda i: w_in[:, o[i]:o[i + 1]]
    z = lambda n: jnp.zeros((d, n), F32)
    qa, ka, va, qb, kb, vb, qk, vc, oc, g, cq, ckv, kr = [col(i) for i in range(13)]
    a_ord = (0, 2, 1, 3)
    sc = HD ** -0.5
    gI = jnp.concatenate([g[:, 0:4], g[:, 8:12], z(120)], axis=1)
    gF = jnp.concatenate([g[:, 4:8], g[:, 12:16], z(120)], axis=1)
    hr = MLA_ROPE // 2
    kr_g = jnp.concatenate([z(64), kr, z(32)], axis=1)
    krr_g = jnp.concatenate([z(64), kr[:, hr:], kr[:, :hr], z(32)], axis=1)
    w1 = jnp.concatenate([
        _head_cols(qa, a_ord, False) * sc, _head_cols(qa, a_ord, True) * sc,
        ka, _head_cols(ka, (0, 1), True), va,
        qb * sc, kb, vb,
        qk, vc, oc, gI, gF,
        cq, z(64), ckv, kr_g, krr_g], axis=1)
    assert w1.shape[1] == _NC
    gb = gate_bias.astype(F32)
    gbias = jnp.stack([jnp.concatenate([gb[0], gb[2], jnp.zeros((120,), F32)]),
                       jnp.concatenate([gb[1], gb[3], jnp.zeros((120,), F32)])])
    dq = MLA_NOPE + MLA_ROPE
    sd = dq ** -0.5
    qh, qrh = [], []
    for h in range(4):
        nope = w_uq[:, h * dq:h * dq + MLA_NOPE]
        rope = w_uq[:, h * dq + MLA_NOPE:(h + 1) * dq]
        zq = lambda n: jnp.zeros((MLA_Q_LORA, n), F32)
        qh.append(jnp.concatenate([nope, rope, zq(32)], axis=1))
        qrh.append(jnp.concatenate([zq(64), rope[:, hr:], rope[:, :hr], zq(32)], axis=1))
    wuq = jnp.concatenate(qh + qrh, axis=1) * (sd * _LOG2E)
    wuq = jnp.concatenate([wuq, jnp.zeros((64, 1024), F32)], axis=0)
    kh = []
    for h in range(4):
        kn = w_ukv[:, h * 128:h * 128 + MLA_NOPE]
        kh.append(jnp.concatenate([kn, jnp.zeros((MLA_KV_LORA, 64), F32)], axis=1))
    vh = [w_ukv[:, h * 128 + MLA_NOPE:(h + 1) * 128] for h in range(4)]
    wukv = jnp.concatenate(kh + vh, axis=1)
    gq = jnp.concatenate([g_q.astype(F32), jnp.zeros((64,), F32)]).reshape(1, 256)
    gkv = g_kv.astype(F32).reshape(1, 128)
    wo = jnp.concatenate([w_out[i * HD:(i + 1) * HD] for i in a_ord] + [w_out[256:]], axis=0)
    return dict(w1=w1.astype(MXU), gbias=gbias, wuq=wuq.astype(MXU), wukv=wukv.astype(MXU),
                gq=gq, gkv=gkv, wo=wo.astype(MXU), conv=conv_w.astype(F32), rpb=rpb.astype(F32))


def _rope_tables(s_len, ctx_len):
    t = jnp.arange(s_len, dtype=jnp.int32)
    rows = (t // GRID_W).astype(F32)[:, None]
    cols = (t % GRID_W).astype(F32)[:, None]

    def ang(rot_dim):
        d_ax = rot_dim // 2
        inv = ROPE_BASE ** (-jnp.arange(0, d_ax, 2, dtype=F32) / d_ax)
        return jnp.concatenate([rows * inv, cols * inv], axis=-1)

    aa = ang(HD)
    ca, sa = jnp.cos(aa), jnp.sin(aa)
    cos_a = jnp.tile(jnp.concatenate([ca, ca], axis=-1), (1, 2))
    sin_a = jnp.tile(jnp.concatenate([-sa, sa], axis=-1), (1, 2))
    ad = ang(MLA_ROPE)
    cd, sd = jnp.cos(ad), jnp.sin(ad)
    one = lambda n: jnp.ones((s_len, n), F32)
    zero = lambda n: jnp.zeros((s_len, n), F32)
    cos_d = jnp.concatenate([one(64), cd, cd, one(32)], axis=-1)
    sin_d = jnp.concatenate([zero(64), -sd, sd, zero(32)], axis=-1)
    pad = lambda a, v: jnp.concatenate([jnp.full((ctx_len, LANES), v, F32), a], axis=0)
    return pad(cos_a, 1.0), pad(sin_a, 0.0), pad(cos_d, 1.0), pad(sin_d, 0.0)


def _mod_kernel(c_ref, w_ref, b_ref, o_ref):
    c = c_ref[...]
    s = c * jax.nn.sigmoid(c)
    w = w_ref[0]
    acc = jnp.zeros(o_ref.shape[1:], F32)
    sp = _split3(s)
    wp = _split3(w)
    for i, j in ((0, 0), (0, 1), (1, 0), (1, 1), (0, 2), (2, 0)):
        acc = acc + _dot(sp[i], wp[j])
    o_ref[0] = acc + b_ref[0]


def _mod_call(c_all, w_ada, b_ada):
    depth, d, n = w_ada.shape
    tn = 768
    return pl.pallas_call(
        _mod_kernel,
        grid=(depth, n // tn),
        in_specs=[pl.BlockSpec((8, d), lambda l, j: (0, 0)),
                  pl.BlockSpec((1, d, tn), lambda l, j: (l, 0, j)),
                  pl.BlockSpec((1, 1, tn), lambda l, j: (l, 0, j))],
        out_specs=pl.BlockSpec((1, 8, tn), lambda l, j: (l, 0, j)),
        out_shape=jax.ShapeDtypeStruct((depth, 8, n), F32),
        compiler_params=_cparams(("parallel", "parallel")),
        name="adaln_mod",
    )(c_all, w_ada, b_ada.reshape(depth, 1, n))


def _rms(x, n=None):
    n = x.shape[-1] if n is None else n
    return x * lax.rsqrt(jnp.sum(x * x, axis=-1, keepdims=True) * (1.0 / n) + EPS)


def _proj_kernel(x_ref, mod_ref, modc_ref, g1_ref, w1_ref, gb_ref, wuq_ref, wukv_ref, gq_ref, gkv_ref,
                 ca_ref, sa_ref, cd_ref, sd_ref,
                 qa_o, ka_o, va_o, qb_o, kb_o, vb_o, qk_o, vc_o, oc_o, gi_o, gf_o, qd_o, kd_o, vd_o, *, tiles_b):
    d = x_ref.shape[-1]
    x = x_ref[...]
    row = lax.broadcasted_iota(jnp.int32, (x.shape[0], 1), 0)
    is_ctx = (pl.program_id(0) % tiles_b == 0) & (row < TILE)
    mod = jnp.where(is_ctx, modc_ref[0][:, 0:2 * d], mod_ref[0][:, 0:2 * d])
    h = _rms(x) * g1_ref[...]
    h = (h * (1.0 + mod[:, d:2 * d]) + mod[:, 0:d]).astype(MXU)
    proj = lambda a, n: _dot(h, w1_ref[:, a:a + n])
    ca, sa, cd, sd = ca_ref[...], sa_ref[...], cd_ref[...], sd_ref[...]

    q, qr = proj(_C_QA, 256), proj(_C_QAR, 256)
    for g in range(2):
        sl = slice(g * LANES, (g + 1) * LANES)
        qa_o[:, sl] = (q[:, sl] * ca + qr[:, sl] * sa).astype(qa_o.dtype)
    ka_o[...] = (proj(_C_KA, 128) * ca + proj(_C_KAR, 128) * sa).astype(ka_o.dtype)
    va_o[...] = proj(_C_VA, 128).astype(va_o.dtype)
    qb_o[...] = proj(_C_QB, 256).astype(qb_o.dtype)
    kb_o[...] = proj(_C_KB, 256).astype(kb_o.dtype)
    vb_o[...] = proj(_C_VB, 256).astype(vb_o.dtype)
    qk_o[...] = proj(_C_QK, 512).astype(qk_o.dtype)
    vc_o[...] = proj(_C_VC, 256).astype(vc_o.dtype)
    oc_o[...] = proj(_C_OC, 256).astype(oc_o.dtype)
    gb = gb_ref[...]
    gi_o[...] = proj(_C_GI, 128) + gb[0:1, :]
    gf_o[...] = proj(_C_GF, 128) + gb[1:2, :]
    cq = proj(_C_CQ, 256)
    cqn = (_rms(cq, MLA_Q_LORA) * gq_ref[...]).astype(MXU)
    q2 = _dot(cqn, wuq_ref[...])
    ckv = proj(_C_CKV, 128)
    ckvn = (_rms(ckv) * gkv_ref[...]).astype(MXU)
    kv2 = _dot(ckvn, wukv_ref[...])
    krope = proj(_C_KR, 128) * cd + proj(_C_KRR, 128) * sd
    for hh in range(4):
        sl = slice(hh * LANES, (hh + 1) * LANES)
        sr = slice(512 + hh * LANES, 512 + (hh + 1) * LANES)
        qd_o[:, sl] = (q2[:, sl] * cd + q2[:, sr] * sd).astype(qd_o.dtype)
        kd_o[:, sl] = (kv2[:, sl] + krope).astype(kd_o.dtype)
    vt = kv2[:, 512:768].T
    ones = jnp.ones((D_VROWS - HD, vt.shape[1]), F32)
    for hh in range(4):
        vd_o[hh * D_VROWS:(hh + 1) * D_VROWS, :] = jnp.concatenate(
            [vt[hh * HD:(hh + 1) * HD, :], ones], axis=0).astype(vd_o.dtype)


def _tile_maps(n_tiles_b, n_batch):
    tok = lambda i: (i, 0)
    pos = lambda i: (i % n_tiles_b, 0)
    modr = lambda i: (jnp.where(i % n_tiles_b == 0, n_batch, i // n_tiles_b), 0, 0)
    return tok, pos, modr


def _proj_call(x_all, mod3, g1, pk, tabs, n_batch):
    t_all, d = x_all.shape
    npb = t_all // n_batch
    rt = PROJ_ROWS if npb % PROJ_ROWS == 0 else TILE
    n_tiles = t_all // rt
    ntb = n_tiles // n_batch
    tok = lambda i: (i, 0)
    pos = lambda i: (i % ntb, 0)
    const = lambda i: (0, 0)
    full = lambda a: pl.BlockSpec(a.shape, const)
    widths = (256, 128, 128, 256, 256, 256, 512, 256, 256, 128, 128, 512, 512)
    dts = (MXU,) * 9 + (F32, F32) + (MXU,) * 2
    vt_spec = pl.BlockSpec((None, 4 * D_VROWS, rt), lambda i: (i // ntb, 0, i % ntb))
    vt_shape = jax.ShapeDtypeStruct((n_batch, 4 * D_VROWS, npb), MXU)
    tab = pl.BlockSpec((rt, LANES), pos)
    modw = mod3.shape[-1]
    return pl.pallas_call(
        functools.partial(_proj_kernel, tiles_b=ntb),
        grid=(n_tiles,),
        in_specs=[pl.BlockSpec((rt, d), tok), pl.BlockSpec((1, 1, modw), lambda i: (i // ntb, 0, 0)),
                  pl.BlockSpec((1, 1, modw), lambda i: (n_batch, 0, 0)), full(g1),
                  full(pk["w1"]), full(pk["gbias"]), full(pk["wuq"]), full(pk["wukv"]),
                  full(pk["gq"]), full(pk["gkv"]), tab, tab, tab, tab],
        out_specs=[pl.BlockSpec((rt, w), tok) for w in widths] + [vt_spec],
        out_shape=[jax.ShapeDtypeStruct((t_all, w), dt) for w, dt in zip(widths, dts)] + [vt_shape],
        compiler_params=_cparams(("parallel",)),
        name="in_proj",
    )(x_all, mod3, mod3, g1, pk["w1"], pk["gbias"], pk["wuq"], pk["wukv"], pk["gq"], pk["gkv"], *tabs)


_A_ORD = (0, 2, 1, 3)
A_BLK = 128
A_WIN = 128


def _mixa_kernel(sink_ref, q_ref, kp_ref, kc_ref, kn_ref, kx_ref, vp_ref, vc_ref, vn_ref, vx_ref, o_ref,
                 *, s_len, ctx_blocks):
    n = pl.program_id(1)
    is_lat = n >= ctx_blocks
    nb = 3 * A_BLK
    lane = lax.broadcasted_iota(jnp.int32, (A_BLK, LANES), 1)
    lo = lane < HD
    qi = lax.broadcasted_iota(jnp.int32, (2 * A_BLK, nb), 0) & (A_BLK - 1)
    kj = lax.broadcasted_iota(jnp.int32, (2 * A_BLK, nb), 1)
    qpos = (n - ctx_blocks) * A_BLK + qi
    kpos = (n - ctx_blocks - 1) * A_BLK + kj
    ok = is_lat & (kpos >= 0) & (kpos < s_len) & (jnp.abs(qpos - kpos) <= A_WIN)
    row = lax.broadcasted_iota(jnp.int32, (2 * A_BLK, 1), 0)
    k_all = jnp.concatenate([kp_ref[...], kc_ref[...], kn_ref[...], kx_ref[...]], axis=0)
    v_all = jnp.concatenate([vp_ref[...], vc_ref[...], vn_ref[...], vx_ref[...]], axis=0)
    q = q_ref[...]
    zero = jnp.zeros((A_BLK, LANES), q.dtype)
    scores = []
    for g in range(2):
        qg = q[:, g * LANES:(g + 1) * LANES]
        q2 = jnp.concatenate([jnp.where(lo, qg, zero), jnp.where(lo, zero, qg)], axis=0)
        scores.append(_dot_nt(q2, k_all))
    for g in range(2):
        s = scores[g]
        s = jnp.concatenate([jnp.where(ok, s[:, :nb], NEG), s[:, nb:]], axis=1)
        sink = jnp.where(row < A_BLK, sink_ref[_A_ORD[2 * g]], sink_ref[_A_ORD[2 * g + 1]])
        m = jnp.maximum(jnp.max(s, axis=-1, keepdims=True), sink)
        p = jnp.exp(s - m)
        den = jnp.sum(p, axis=-1, keepdims=True) + jnp.exp(sink - m)
        pv = _dot(p.astype(MXU), v_all) / den
        o_ref[:, g * LANES:(g + 1) * LANES] = jnp.where(lo, pv[:A_BLK], pv[A_BLK:]).astype(o_ref.dtype)


def _mixa_call(qa, ka, va, sink, n_batch, s_len):
    t_all = qa.shape[0]
    nq = t_all // n_batch // A_BLK
    cb = TILE // A_BLK
    qmap = lambda b, n: (b * nq + n, 0)
    pmap = lambda b, n: (b * nq + jnp.clip(n - 1, cb, nq - 1), 0)
    nmap = lambda b, n: (b * nq + jnp.clip(n + 1, cb, nq - 1), 0)
    xmap = lambda b, n: (b * (nq // cb), 0)
    kv = lambda m: pl.BlockSpec((A_BLK, LANES), m)
    kvx = pl.BlockSpec((TILE, LANES), xmap)
    return pl.pallas_call(
        functools.partial(_mixa_kernel, s_len=s_len, ctx_blocks=cb),
        grid=(n_batch, nq),
        in_specs=[pl.BlockSpec(memory_space=pltpu.SMEM), pl.BlockSpec((A_BLK, 256), qmap),
                  kv(pmap), kv(qmap), kv(nmap), kvx, kv(pmap), kv(qmap), kv(nmap), kvx],
        out_specs=pl.BlockSpec((A_BLK, 256), qmap),
        out_shape=jax.ShapeDtypeStruct((t_all, 256), MXU),
        compiler_params=_cparams(("parallel", "parallel")),
        name="mixer_a",
    )(sink.astype(F32), qa, ka, ka, ka, ka, va, va, va, va)


def _nat_bias_table(rpb):
    qc = jnp.arange(GRID_W)[:, None]
    kc = jnp.arange(GRID_W)[None, :]
    cs = jnp.clip(qc - NAT_KC // 2, 0, GRID_W - NAT_KC)
    ok = (kc >= cs) & (kc < cs + NAT_KC)
    dc = jnp.clip(kc - qc + NAT_KC - 1, 0, 2 * NAT_KC - 2)
    t = jnp.where(ok[None, None], rpb[:, :, dc], NEG)
    pair = lambda h: jnp.concatenate([t[h, :-1], t[h, 1:]], axis=-1)
    return jnp.stack([jnp.concatenate([pair(2 * g), pair(2 * g + 1)], axis=1) for g in range(2)])


def _mixb_kernel(q_ref, k_ref, v_ref, tb_ref, o_ref, *, rows):
    j = pl.program_id(1)
    is_ctx = j == 0
    nloc = NAT_KR * GRID_W
    lane = lax.broadcasted_iota(jnp.int32, (GRID_W, LANES), 1)
    lo = lane < HD
    zero = jnp.zeros((GRID_W, LANES), q_ref.dtype)
    for i in range(TILE // GRID_W):
        r = jnp.maximum((j - 1) * (TILE // GRID_W) + i, 0)
        r0 = jnp.clip(r - NAT_KR // 2, 0, rows - NAT_KR)
        kstart = pl.multiple_of(TILE + r0 * GRID_W, GRID_W)
        dr0 = r0 - r + NAT_KR - 1
        for g in range(2):
            sl = slice(g * LANES, (g + 1) * LANES)
            qg = q_ref[i * GRID_W:(i + 1) * GRID_W, sl]
            q2 = jnp.concatenate([jnp.where(lo, qg, zero), jnp.where(lo, zero, qg)], axis=0)
            s_loc = _dot_nt(q2, k_ref[pl.ds(kstart, nloc), sl])
            s_ctx = _dot_nt(q2, k_ref[0:TILE, sl])
            bias = jnp.concatenate([tb_ref[g, dr0 + 2 * t] for t in range(NAT_KR // 2)], axis=1)
            s_loc = jnp.where(is_ctx, NEG, s_loc + bias)
            m = jnp.maximum(jnp.max(s_loc, axis=-1, keepdims=True), jnp.max(s_ctx, axis=-1, keepdims=True))
            p_loc = jnp.exp(s_loc - m)
            p_ctx = jnp.exp(s_ctx - m)
            den = jnp.sum(p_loc, axis=-1, keepdims=True) + jnp.sum(p_ctx, axis=-1, keepdims=True)
            pv = _dot(p_loc.astype(MXU), v_ref[pl.ds(kstart, nloc), sl]) + _dot(p_ctx.astype(MXU), v_ref[0:TILE, sl])
            pv = pv / den
            o_ref[i * GRID_W:(i + 1) * GRID_W, sl] = jnp.where(lo, pv[:GRID_W], pv[GRID_W:]).astype(o_ref.dtype)


def _mixb_call(qb, kb, vb, tb, n_batch, s_len):
    t_all = qb.shape[0]
    npb = t_all // n_batch
    ntb = npb // TILE
    qmap = lambda b, j: (b * ntb + j, 0)
    kvmap = lambda b, j: (b, 0, 0)
    kvspec = pl.BlockSpec((None, npb, 256), kvmap)
    return pl.pallas_call(
        functools.partial(_mixb_kernel, rows=s_len // GRID_W),
        grid=(n_batch, ntb),
        in_specs=[pl.BlockSpec((TILE, 256), qmap), kvspec, kvspec,
                  pl.BlockSpec(tb.shape, lambda b, j: (0, 0, 0, 0))],
        out_specs=pl.BlockSpec((TILE, 256), qmap),
        out_shape=jax.ShapeDtypeStruct((t_all, 256), MXU),
        compiler_params=_cparams(("parallel", "arbitrary")),
        name="mixer_b",
    )(qb, kb.reshape(n_batch, npb, 256), vb.reshape(n_batch, npb, 256), tb)


D_TK = 256
D_UNROLL = 8


def _mixd_kernel(q_ref, k_ref, vt_ref, o_ref, acc_sc):
    j = pl.program_id(1)
    n_lat = (pl.num_programs(1) - 1) * TILE
    n_rest = jnp.where(j == 0, 0, n_lat // (D_TK * D_UNROLL))
    acc_sc[...] = jnp.zeros(acc_sc.shape, F32)
    qs = [q_ref[:, h * LANES:(h + 1) * LANES] for h in range(4)]

    def steps(starts, tk, ms):
        ms = list(ms)
        sts = [[_dot_nt(k_ref[pl.ds(ks, tk), h * LANES:(h + 1) * LANES], qs[h]) for h in range(4)]
               for ks in starts]
        for ks, st4 in zip(starts, sts):
            for h in range(4):
                st = st4[h].astype(MXU)
                m_new = jnp.maximum(ms[h], jnp.max(st, axis=0, keepdims=True).astype(F32))
                alpha = jnp.exp2(ms[h] - m_new)
                pt = jnp.exp2(st - m_new.astype(MXU))
                ms[h] = m_new
                pv = _dot(vt_ref[h * D_VROWS:(h + 1) * D_VROWS, pl.ds(ks, tk)], pt)
                acc_sc[h] = acc_sc[h] * alpha + pv
        return tuple(ms)

    init = tuple(jnp.full((1, TILE), NEG, F32) for _ in range(4))
    carry = steps([0], TILE, init)
    body = lambda i, c: steps([pl.multiple_of(TILE + (D_UNROLL * i + u) * D_TK, TILE) for u in range(D_UNROLL)],
                              D_TK, c)
    lax.fori_loop(0, n_rest, body, carry)
    for p in range(2):
        outs = [acc_sc[h, 0:HD, :] / acc_sc[h, HD:HD + 1, :] for h in (2 * p, 2 * p + 1)]
        o_ref[:, p * LANES:(p + 1) * LANES] = jnp.concatenate(outs, axis=0).T.astype(o_ref.dtype)


def _mixd_call(qd, kd, vdt, n_batch):
    t_all = qd.shape[0]
    npb = t_all // n_batch
    ntb = npb // TILE
    assert ((ntb - 1) * TILE) % (D_TK * D_UNROLL) == 0
    qmap = lambda b, j: (b * ntb + j, 0)
    kvmap = lambda b, j: (b, 0, 0)
    return pl.pallas_call(
        _mixd_kernel,
        grid=(n_batch, ntb),
        in_specs=[pl.BlockSpec((TILE, 512), qmap), pl.BlockSpec((None, npb, 512), kvmap),
                  pl.BlockSpec((None, 4 * D_VROWS, npb), kvmap)],
        out_specs=pl.BlockSpec((TILE, 256), qmap),
        out_shape=jax.ShapeDtypeStruct((t_all, 256), MXU),
        scratch_shapes=[pltpu.VMEM((4, D_VROWS, TILE), F32)],
        compiler_params=_cparams(("parallel", "arbitrary")),
        name="mixer_d",
    )(qd, kd.reshape(n_batch, npb, 512), vdt)


C_L = 128
HALO = 16


def _log_sigmoid(x):
    return jnp.minimum(x, 0.0) - jnp.log1p(jnp.exp(-jnp.abs(x)))


def _mlstm_prep(d, j, cb, nch, qk_ref, hp_ref, hn_ref, v_ref, gi_ref, gf_ref, conv_ref, c_sc, n_sc):
    row = lax.broadcasted_iota(jnp.int32, (C_L, 1), 0)
    rr = lax.broadcasted_iota(jnp.int32, (C_L, C_L), 0)
    cc = lax.broadcasted_iota(jnp.int32, (C_L, C_L), 1)
    lo = cc < HD
    causal = (cc <= rr) if d == 0 else (cc >= rr)

    prev_ok = (j != 0) & (j != cb)
    next_ok = (j != cb - 1) & (j != nch - 1)
    x = qk_ref[...].astype(F32)
    prow = jnp.where(prev_ok, hp_ref[HALO - 1:HALO, :].astype(F32), 0.0)
    nrow = jnp.where(next_ok, hn_ref[0:1, :].astype(F32), 0.0)
    xm1 = jnp.where(row == 0, prow, pltpu.roll(x, 1, 0))
    xp1 = jnp.where(row == C_L - 1, nrow, pltpu.roll(x, C_L - 1, 0))
    w = conv_ref[...]
    u = xm1 * w[0:1, :] + x * w[1:2, :] + xp1 * w[2:3, :]
    a = u * jax.nn.sigmoid(u)
    q_all = a[:, 0:256].astype(MXU)
    k_all = a[:, 256:512] * (HD ** -0.5)

    f = _log_sigmoid(gf_ref[...])
    bc = sum(_dot(causal.astype(MXU), part) for part in _split3(f))
    g = gi_ref[...] - bc
    st = dict(d=d, row=row, lo=lo, causal=causal, blockdiag=(rr < HD) == (cc < HD), bc=bc, g=g, gt=g.T,
              last=C_L - 1 if d == 0 else 0, units=[], pairs=[])
    for p in range(2):
        sl = slice(p * LANES, (p + 1) * LANES)
        qp = q_all[:, sl]
        kp32 = k_all[:, sl]
        kp = kp32.astype(MXU)
        cmat, nmat = c_sc[d, p], n_sc[d, p]
        c16, n16 = cmat.astype(MXU), nmat.astype(MXU)
        zero = jnp.zeros_like(qp)
        st["pairs"].append(dict(sl=sl, kp32=kp32, vp=v_ref[:, sl], cmat=cmat, nmat=nmat))
        for hh in range(2):
            qh = jnp.where(lo, qp, zero) if hh == 0 else jnp.where(lo, zero, qp)
            st["units"].append(dict(ln=d * 4 + 2 * p + hh, s=_dot_nt(qh, kp), qc=_dot(qh, c16), qn=_dot(qh, n16)))
    return st


def _mlstm_gates(st, m_sc):
    causal, bc, g, gt, last = st["causal"], st["bc"], st["g"], st["gt"], st["last"]
    for un in st["units"]:
        ln = un["ln"]
        bcol = bc[:, ln:ln + 1]
        logd = jnp.where(causal, bcol + gt[ln:ln + 1, :], NEG)
        mloc = jnp.max(logd, axis=-1, keepdims=True)
        m_prev = m_sc[ln:ln + 1, 0:1]
        m_t = jnp.maximum(bcol + m_prev, mloc)
        sw = un["s"] * jnp.exp(logd - m_t)
        a_end = mloc[last:last + 1, :]
        b_end = bcol[last:last + 1, :]
        m_new = jnp.maximum(b_end + m_prev, a_end)
        un.update(sw=sw, m_t=m_t, inter=jnp.exp(bcol + m_prev - m_t), m_new=m_new,
                  decay=jnp.exp(b_end + m_prev - m_new), scl=jnp.exp(a_end - m_new),
                  wcol=jnp.exp(g[:, ln:ln + 1] + (b_end - a_end)))


def _mlstm_finish(st, o_ref, c_sc, n_sc, m_sc):
    d, lo, row = st["d"], st["lo"], st["row"]
    for p, pr in enumerate(st["pairs"]):
        us = st["units"][2 * p:2 * p + 2]
        nums = [_dot(un["sw"].astype(MXU), pr["vp"]) for un in us]
        kwt = (pr["kp32"] * jnp.where(lo, us[0]["wcol"], us[1]["wcol"])).T.astype(MXU)
        kv = jnp.where(st["blockdiag"], _dot(kwt, pr["vp"]), 0.0)
        ksum = _dot(kwt, jnp.ones((C_L, LANES), MXU))
        houts = []
        for un, num in zip(us, nums):
            den = jnp.sum(un["sw"], axis=-1, keepdims=True) + un["inter"] * un["qn"]
            houts.append((num + un["inter"] * un["qc"]) / jnp.maximum(jnp.abs(den), jnp.exp(-un["m_t"])))
            m_sc[un["ln"]:un["ln"] + 1, :] = jnp.broadcast_to(un["m_new"], (1, LANES))
        o_ref[:, pr["sl"]] = jnp.where(lo, houts[0], houts[1])
        dec = jnp.where(row < HD, us[0]["decay"], us[1]["decay"])
        scl = jnp.where(row < HD, us[0]["scl"], us[1]["scl"])
        c_sc[d, p] = dec * pr["cmat"] + scl * kv
        n_sc[d, p] = dec * pr["nmat"] + scl * ksum


def _mixc_kernel(qkf, hpf, hnf, vf, gif, gff, qkb, hpb, hnb, vb, gib, gfb, conv_ref, of_ref, ob_ref,
                 c_sc, n_sc, m_sc, *, cb, nch):
    i = pl.program_id(1)

    @pl.when(i == 0)
    def _():
        c_sc[...] = jnp.zeros(c_sc.shape, F32)
        n_sc[...] = jnp.zeros(n_sc.shape, F32)
        m_sc[...] = jnp.zeros(m_sc.shape, F32)

    jb = jnp.where(i < cb, cb - 1 - i, nch + cb - 1 - i)
    sf = _mlstm_prep(0, i, cb, nch, qkf, hpf, hnf, vf, gif, gff, conv_ref, c_sc, n_sc)
    sb = _mlstm_prep(1, jb, cb, nch, qkb, hpb, hnb, vb, gib, gfb, conv_ref, c_sc, n_sc)
    _mlstm_gates(sf, m_sc)
    _mlstm_gates(sb, m_sc)
    _mlstm_finish(sf, of_ref, c_sc, n_sc, m_sc)
    _mlstm_finish(sb, ob_ref, c_sc, n_sc, m_sc)


def _mixc_call(qk, vc, gi, gf, conv_w, n_batch):
    t_all = qk.shape[0]
    nch = t_all // n_batch // C_L
    cb = TILE // C_L
    hb = C_L // HALO
    n_halo = t_all // HALO
    jf = lambda i: i
    jb = lambda i: jnp.where(i < cb, cb - 1 - i, nch + cb - 1 - i)

    def specs(jmap):
        cur = lambda b, i: (b * nch + jmap(i), 0)
        prv = lambda b, i: (jnp.maximum((b * nch + jmap(i)) * hb - 1, 0), 0)
        nxt = lambda b, i: (jnp.minimum((b * nch + jmap(i) + 1) * hb, n_halo - 1), 0)
        return cur, [pl.BlockSpec((C_L, 512), cur), pl.BlockSpec((HALO, 512), prv), pl.BlockSpec((HALO, 512), nxt),
                     pl.BlockSpec((C_L, 256), cur), pl.BlockSpec((C_L, LANES), cur), pl.BlockSpec((C_L, LANES), cur)]

    cur_f, in_f = specs(jf)
    cur_b, in_b = specs(jb)
    args = (qk, qk, qk, vc, gi, gf)
    return pl.pallas_call(
        functools.partial(_mixc_kernel, cb=cb, nch=nch),
        grid=(n_batch, nch),
        in_specs=in_f + in_b + [pl.BlockSpec(conv_w.shape, lambda b, i: (0, 0))],
        out_specs=[pl.BlockSpec((C_L, 256), cur_f), pl.BlockSpec((C_L, 256), cur_b)],
        out_shape=[jax.ShapeDtypeStruct((t_all, 256), F32)] * 2,
        scratch_shapes=[pltpu.VMEM((2, 2, LANES, LANES), F32), pltpu.VMEM((2, 2, LANES, LANES), F32),
                        pltpu.VMEM((8, LANES), F32)],
        compiler_params=_cparams(("parallel", "arbitrary")),
        name="mixer_c",
    )(*args, *args, conv_w)


def _out_kernel(x_ref, mod_ref, ya_ref, yb_ref, hf_ref, hb_ref, oc_ref, yd_ref, wo_ref, g2_ref, wr_ref, br_ref,
                x1_o, h2_o, sel_o, selt_o, gate_o, meta_o, carry_sc):
    d = x_ref.shape[-1]
    i = pl.program_id(0)

    @pl.when(i == 0)
    def _():
        carry_sc[...] = jnp.zeros(carry_sc.shape, F32)

    mod = mod_ref[0]
    ym = (jax.nn.sigmoid(oc_ref[...].astype(F32)) * (hf_ref[...] + hb_ref[...])).astype(MXU)
    acc = (_dot(ya_ref[...], wo_ref[0:256, :]) + _dot(yb_ref[...], wo_ref[256:512, :])
           + _dot(ym, wo_ref[512:768, :]) + _dot(yd_ref[...], wo_ref[768:1024, :]))
    x1 = x_ref[...] + mod[:, 2 * d:3 * d] * acc
    x1_o[...] = x1
    h2 = _rms(x1) * g2_ref[...]
    h2 = h2 * (1.0 + mod[:, 4 * d:5 * d]) + mod[:, 3 * d:4 * d]
    h2_o[...] = h2.astype(h2_o.dtype)

    hp = _split3(h2)
    wp = _split3(wr_ref[...])
    logits = br_ref[...]
    for a, b in ((0, 0), (0, 1), (1, 0), (1, 1), (0, 2), (2, 0)):
        logits = logits + _dot(hp[a], wp[b])
    lane = lax.broadcasted_iota(jnp.int32, (TILE, LANES), 1)
    vals, idxs = [], []
    cur = logits
    for _ in range(TOP_K):
        mk = jnp.max(cur, axis=-1, keepdims=True)
        ik = jnp.min(jnp.where(cur == mk, lane, LANES), axis=-1, keepdims=True)
        cur = jnp.where(lane == ik, NEG, cur)
        vals.append(mk)
        idxs.append(ik)
    es = [jnp.exp(v - vals[0]) for v in vals]
    esum = es[0] + es[1] + es[2] + es[3]
    hot = [(lane == ik) for ik in idxs]
    multi = sum(h.astype(F32) for h in hot)
    rr = lax.broadcasted_iota(jnp.int32, (TILE, TILE), 0)
    cc = lax.broadcasted_iota(jnp.int32, (TILE, TILE), 1)
    before = _dot((cc < rr).astype(MXU), multi.astype(MXU))
    tile_cnt = jnp.sum(multi, axis=0, keepdims=True)
    sel_t = jnp.zeros((TILE, LANES), F32)
    gate_t = jnp.zeros((TILE, LANES), F32)
    for k in range(TOP_K):
        lr = jnp.sum(jnp.where(hot[k], before, 0.0), axis=-1, keepdims=True)
        sel_t = jnp.where(lane == k, idxs[k].astype(F32), sel_t)
        sel_t = jnp.where(lane == TOP_K + k, lr, sel_t)
        gate_t = jnp.where(lane == k, es[k] / esum, gate_t)
    sel_o[...] = sel_t.astype(jnp.int32)
    selt_o[0] = sel_t.T[0:8, :].astype(jnp.int32)
    gate_o[...] = gate_t
    row8 = lax.broadcasted_iota(jnp.int32, (8, LANES), 0)
    meta = jnp.where(row8 == 0, carry_sc[...], jnp.where(row8 == 1, tile_cnt, 0.0))
    meta_o[0] = meta.astype(jnp.int32)
    carry_sc[...] = carry_sc[...] + jnp.ceil(tile_cnt * (1.0 / RUN_ALIGN)) * RUN_ALIGN


def _out_call(x_all, mod3, ya, yb, hf, hb, oc, yd, wo, g2, wr, br, n_batch):
    t_all, d = x_all.shape
    n_tiles = t_all // TILE
    tok, _, modr = _tile_maps(n_tiles // n_batch, n_batch)
    const = lambda i: (0, 0)
    full = lambda a: pl.BlockSpec(a.shape, const)
    t256 = pl.BlockSpec((TILE, 256), tok)
    td = pl.BlockSpec((TILE, d), tok)
    tl = pl.BlockSpec((TILE, LANES), tok)
    return pl.pallas_call(
        _out_kernel,
        grid=(n_tiles,),
        in_specs=[td, pl.BlockSpec((1, 1, mod3.shape[-1]), modr), t256, t256, t256, t256, t256, t256,
                  full(wo), full(g2), full(wr), full(br)],
        out_specs=[td, td, tl, pl.BlockSpec((1, 8, TILE), lambda i: (i, 0, 0)), tl,
                   pl.BlockSpec((1, 8, LANES), lambda i: (i, 0, 0))],
        out_shape=[jax.ShapeDtypeStruct((t_all, d), F32), jax.ShapeDtypeStruct((t_all, d), MXU),
                   jax.ShapeDtypeStruct((t_all, LANES), jnp.int32), jax.ShapeDtypeStruct((n_tiles, 8, TILE), jnp.int32),
                   jax.ShapeDtypeStruct((t_all, LANES), F32), jax.ShapeDtypeStruct((n_tiles, 8, LANES), jnp.int32)],
        scratch_shapes=[pltpu.VMEM((1, LANES), F32)],
        compiler_params=_cparams(("arbitrary",)),
        name="out_proj_router",
    )(x_all, mod3, ya, yb, hf, hb, oc, yd, wo, g2, wr, br)


SORT_R = 48
SORT_W = N_EXP * SORT_R
_M_CNT, _M_ROUNDS = N_EXP, 2 * N_EXP


def _route_plan(meta, t_all):
    base = meta[:, 0, :N_EXP]
    cnt = meta[:, 1, :N_EXP]
    total = base[-1] + (cnt[-1] + RUN_ALIGN - 1) // RUN_ALIGN * RUN_ALIGN
    padded = (total + SORT_R + MOE_BLK - 1) // MOE_BLK * MOE_BLK
    pad_end = jnp.cumsum(padded)
    pad_start = pad_end - padded
    rounds = jnp.maximum((jnp.max(cnt, axis=1, keepdims=True) + SORT_R - 1) // SORT_R, 1)
    plan = jnp.concatenate([pad_start[None, :] + base, cnt, rounds,
                            jnp.zeros((meta.shape[0], LANES - 2 * N_EXP - 1), jnp.int32)], axis=1)
    n_runs = meta.shape[0] * N_EXP
    n_blk = (t_all * TOP_K + n_runs * (RUN_ALIGN - 1) + N_EXP * (SORT_R + MOE_BLK - 1) + MOE_BLK - 1) // MOE_BLK
    blk_row = jnp.arange(n_blk, dtype=jnp.int32)[:, None] * MOE_BLK
    blk_exp = jnp.minimum(jnp.sum((blk_row >= pad_end[None, :]).astype(jnp.int32), axis=1), N_EXP - 1)
    n_used = (pad_end[-1] // MOE_BLK).astype(jnp.int32).reshape(1)
    last_rounds = jnp.maximum((cnt[-1] + SORT_R - 1) // SORT_R, 1)
    tail_start = pad_start + base[-1] + SORT_R * last_rounds
    tail = jnp.concatenate([tail_start, (pad_end - tail_start) // TAIL_ROWS,
                            pad_end[-1:], n_blk - pad_end[-1:] // MOE_BLK,
                            jnp.zeros((LANES - 2 * N_EXP - 2,), jnp.int32)]).astype(jnp.int32).reshape(1, 1, LANES)
    return plan.astype(jnp.int32)[:, None, :], tail, blk_exp.astype(jnp.int32), n_used, n_blk


U32 = jnp.uint32
_HI16 = 0xFFFF0000


def _pack_pairs(x):
    half = x.shape[1] // 2
    bits = lax.bitcast_convert_type(x, U32)
    return (bits[:, half:] & U32(_HI16)) | (bits[:, :half] >> 16)


def _unpack_pairs(w):
    lo = lax.bitcast_convert_type(w << 16, F32).astype(MXU)
    hi = lax.bitcast_convert_type(w & U32(_HI16), F32).astype(MXU)
    return jnp.concatenate([lo, hi], axis=1)


def _run_start(plan_ref, e, r):
    return pl.multiple_of(plan_ref[0, 0, e] + r * SORT_R, RUN_ALIGN)


def _extra_round_copies(plan_ref, r, make_copy):
    for go in (lambda c: c.start(), lambda c: c.wait()):
        for e in range(N_EXP):
            @pl.when(plan_ref[0, 0, _M_CNT + e] > r * SORT_R)
            def _():
                go(make_copy(e, _run_start(plan_ref, e, r)))


def _round_cols(sel_e, sel_lr, r):
    lo = r * SORT_R
    ok = (sel_lr >= lo) & (sel_lr < lo + SORT_R)
    return jnp.where(ok, sel_e * SORT_R + sel_lr - lo, -1)


TAIL_ROWS = 8


def _dispatch_kernel(plan_ref, tail_ref, selt_ref, h2_ref, xs_o, stage, zrows, sem, zsem, pending):
    i = pl.program_id(0)
    last = pl.num_programs(0) - 1
    slot = i % 2
    h2 = h2_ref[...]
    rows = lax.broadcasted_iota(jnp.int32, (TILE, TILE), 0)

    def run_copy(buf, e, dst):
        return pltpu.make_async_copy(stage.at[buf, pl.ds(e * SORT_R, SORT_R), :],
                                     xs_o.at[pl.ds(dst, SORT_R), :], sem.at[buf, e])

    def wait_round0(buf):
        for e in range(N_EXP):
            run_copy(buf, e, 0).wait()

    def sort_round(r):
        cols = [_round_cols(selt_ref[0, k:k + 1, :], selt_ref[0, TOP_K + k:TOP_K + k + 1, :], r)
                for k in range(TOP_K)]
        for c in range(SORT_W // TILE):
            hit = jnp.zeros((TILE, TILE), F32)
            for k in range(TOP_K):
                hit = jnp.where(rows + c * TILE == cols[k], 1.0, hit)
            stage[slot, c * TILE:(c + 1) * TILE, :] = _pack_pairs(_dot(hit.astype(MXU), h2))

    @pl.when(i == 0)
    def _():
        pending[0] = 0

    sort_round(0)

    def start_all(after_previous):
        for e in range(N_EXP):
            if after_previous:
                run_copy(1 - slot, e, 0).wait()
            run_copy(slot, e, _run_start(plan_ref, e, 0)).start(priority=e % 2)

    @pl.when(pending[0] == 1)
    def _():
        start_all(True)

    @pl.when(pending[0] == 0)
    def _():
        start_all(False)

    pending[0] = 1
    rounds = plan_ref[0, 0, _M_ROUNDS]

    @pl.when((rounds > 1) | (i == last))
    def _():
        wait_round0(slot)
        pending[0] = 0

    def extra_round(r, carry):
        sort_round(r)
        _extra_round_copies(plan_ref, r, lambda e, dst: run_copy(slot, e, dst))
        return carry

    lax.fori_loop(1, rounds, extra_round, 0)

    @pl.when(i == last)
    def _():
        zrows[...] = jnp.zeros(zrows.shape, zrows.dtype)
        for go in (lambda c: c.start(), lambda c: c.wait()):
            for e in range(N_EXP):
                def fill(t, carry, e=e):
                    dst = pl.multiple_of(tail_ref[0, 0, e] + t * TAIL_ROWS, TAIL_ROWS)
                    go(pltpu.make_async_copy(zrows.at[pl.ds(0, TAIL_ROWS), :], xs_o.at[pl.ds(dst, TAIL_ROWS), :], zsem))
                    return carry
                lax.fori_loop(0, tail_ref[0, 0, _M_CNT + e], fill, 0)

            def fill_block(t, carry):
                dst = pl.multiple_of(tail_ref[0, 0, _M_ROUNDS] + t * MOE_BLK, MOE_BLK)
                go(pltpu.make_async_copy(zrows, xs_o.at[pl.ds(dst, MOE_BLK), :], zsem))
                return carry
            lax.fori_loop(0, tail_ref[0, 0, _M_ROUNDS + 1], fill_block, 0)


def _dispatch_call(plan, tail, selt, h2, cap):
    t_all, d = h2.shape
    n_tiles = t_all // TILE
    return pl.pallas_call(
        _dispatch_kernel,
        grid=(n_tiles,),
        in_specs=[pl.BlockSpec((1, 1, LANES), lambda i: (i, 0, 0), memory_space=pltpu.SMEM),
                  pl.BlockSpec((1, 1, LANES), lambda i: (0, 0, 0), memory_space=pltpu.SMEM),
                  pl.BlockSpec((1, 8, TILE), lambda i: (i, 0, 0)),
                  pl.BlockSpec((TILE, d), lambda i: (i, 0))],
        out_specs=pl.BlockSpec(memory_space=pl.ANY),
        out_shape=jax.ShapeDtypeStruct((cap, d // 2), U32),
        scratch_shapes=[pltpu.VMEM((2, SORT_W, d // 2), U32), pltpu.VMEM((MOE_BLK, d // 2), U32),
                        pltpu.SemaphoreType.DMA((2, N_EXP)), pltpu.SemaphoreType.DMA(()), pltpu.SMEM((1,), jnp.int32)],
        compiler_params=_cparams(("arbitrary",)),
        name="moe_dispatch",
    )(plan, tail, selt, h2)


GU_GRP = 256


def _regroup_kernel(w_ref, o_ref):
    rr = lax.broadcasted_iota(jnp.int32, (GU_GRP, GU_GRP), 0)
    cc = lax.broadcasted_iota(jnp.int32, (GU_GRP, GU_GRP), 1)
    half = GU_GRP // 2
    perm = (rr == jnp.where(cc < half, 2 * cc, 2 * (cc - half) + 1)).astype(MXU)
    for s in range(w_ref.shape[-1] // GU_GRP):
        sl = slice(s * GU_GRP, (s + 1) * GU_GRP)
        o_ref[:, sl] = _dot(w_ref[:, sl].astype(MXU), perm).astype(o_ref.dtype)


def _regroup_call(w_gu):
    n_exp, d, n2 = w_gu.shape
    tn = 512
    spec = pl.BlockSpec((None, d, tn), lambda e, j: (e, 0, j))
    return pl.pallas_call(
        _regroup_kernel,
        grid=(n_exp, n2 // tn),
        in_specs=[spec],
        out_specs=spec,
        out_shape=jax.ShapeDtypeStruct((n_exp, d, n2), MXU),
        compiler_params=_cparams(("parallel", "parallel")),
        name="regroup_gate_up",
    )(w_gu)


def _regroup_bias(b_gu):
    n_exp, n2 = b_gu.shape
    b = b_gu.astype(F32).reshape(n_exp, n2 // GU_GRP, GU_GRP // 2, 2)
    return jnp.concatenate([b[..., 0], b[..., 1]], axis=-1).reshape(n_exp, 1, n2)


def _expert_kernel(be_ref, nu_ref, xs_ref, wgu_ref, bgu_ref, wd_ref, bd_ref, y_ref):
    n = pl.program_id(0)
    half = GU_GRP // 2

    @pl.when(n < nu_ref[0])
    def _():
        xb = _unpack_pairs(xs_ref[...])
        gu = _dot(xb, wgu_ref[...]) + bgu_ref[...]
        acts = []
        for s in range(gu.shape[-1] // GU_GRP):
            g = jnp.minimum(gu[:, s * GU_GRP:s * GU_GRP + half], SWIGLU_LIMIT)
            u = jnp.clip(gu[:, s * GU_GRP + half:(s + 1) * GU_GRP], -SWIGLU_LIMIT, SWIGLU_LIMIT)
            acts.append(((u + 1.0) * g * jax.nn.sigmoid(SWIGLU_ALPHA * g)).astype(MXU))
        y = _dot(jnp.concatenate(acts, axis=1), wd_ref[...]) + bd_ref[...]
        y_ref[...] = _pack_pairs(y.astype(jnp.bfloat16).astype(F32))

    @pl.when(n >= nu_ref[0])
    def _():
        y_ref[...] = jnp.zeros(y_ref.shape, y_ref.dtype)


def _expert_call(blk_exp, n_used, xs, wgu, bgu, wd, bd):
    cap, dw = xs.shape
    d = 2 * dw
    dff2 = wgu.shape[-1]
    n_blk = cap // MOE_BLK
    emap = lambda n, be, nu: (be[n], 0, 0)
    rows = lambda n, be, nu: (n, 0)
    used_rows = lambda n, be, nu: (jnp.where(n < nu[0], n, 0), 0)
    return pl.pallas_call(
        _expert_kernel,
        grid_spec=pltpu.PrefetchScalarGridSpec(
            num_scalar_prefetch=2,
            grid=(n_blk,),
            in_specs=[pl.BlockSpec((MOE_BLK, dw), used_rows),
                      pl.BlockSpec((None, d, dff2), emap), pl.BlockSpec((None, 1, dff2), emap),
                      pl.BlockSpec((None, dff2 // 2, d), emap), pl.BlockSpec((None, 1, d), emap)],
            out_specs=pl.BlockSpec((MOE_BLK, dw), rows)),
        out_shape=jax.ShapeDtypeStruct((cap, dw), U32),
        compiler_params=_cparams(("arbitrary",)),
        name="expert_ffn",
    )(blk_exp, n_used, xs, wgu, bgu, wd, bd)


COMBINE_AHEAD = 2


def _combine_kernel(plan_ref, next_ref, ahead_ref, sel_ref, gate_ref, x1_ref, mod_ref, y_hbm, x2_o, stage, sem,
                    gf_ref=None):
    d = x1_ref.shape[-1]
    i = pl.program_id(0)
    n_buf = COMBINE_AHEAD + 1
    slot = i % n_buf
    sel = sel_ref[...]
    gate = gate_ref[...]
    lanes = lax.broadcasted_iota(jnp.int32, (TILE, TILE), 1)

    def run_copy(buf, e, src):
        return pltpu.make_async_copy(y_hbm.at[pl.ds(src, SORT_R), :],
                                     stage.at[buf, pl.ds(e * SORT_R, SORT_R), :], sem.at[buf])

    def weighted_sum(r, acc):
        cols = [_round_cols(sel[:, k:k + 1], sel[:, TOP_K + k:TOP_K + k + 1], r) for k in range(TOP_K)]
        for c in range(SORT_W // TILE):
            w = jnp.zeros((TILE, TILE), F32)
            for k in range(TOP_K):
                w = jnp.where(lanes + c * TILE == cols[k], gate[:, k:k + 1], w)
            acc = acc + _dot(w.astype(MXU), _unpack_pairs(stage[slot, c * TILE:(c + 1) * TILE, :]))
        return acc

    def fetch(buf, pref):
        for e in range(N_EXP):
            run_copy(buf, e, _run_start(pref, e, 0)).start(priority=e % 2)

    @pl.when(i == 0)
    def _():
        fetch(0, plan_ref)

    @pl.when((i == 0) & (pl.num_programs(0) > 1))
    def _():
        fetch(1, next_ref)

    @pl.when(i + COMBINE_AHEAD < pl.num_programs(0))
    def _():
        fetch((i + COMBINE_AHEAD) % n_buf, ahead_ref)

    for e in range(N_EXP):
        run_copy(slot, e, 0).wait()
    f = weighted_sum(0, jnp.zeros((TILE, d), F32))

    def extra_round(r, acc):
        _extra_round_copies(plan_ref, r, lambda e, src: run_copy(slot, e, src))
        return weighted_sum(r, acc)

    f = lax.fori_loop(1, plan_ref[0, 0, _M_ROUNDS], extra_round, f)
    x2 = x1_ref[...] + mod_ref[0][:, 5 * d:6 * d] * f
    if gf_ref is None:
        x2_o[...] = x2
    else:
        x2_o[...] = _rms(x2) * gf_ref[...]


def _combine_final_kernel(plan_ref, next_ref, ahead_ref, sel_ref, gate_ref, x1_ref, mod_ref, gf_ref, y_hbm,
                          out_o, stage, sem):
    _combine_kernel(plan_ref, next_ref, ahead_ref, sel_ref, gate_ref, x1_ref, mod_ref, y_hbm, out_o, stage, sem,
                    gf_ref=gf_ref)


def _combine_call(plan, sel, gate, x1, mod3, y, n_batch, g_final=None):
    t_all, d = x1.shape
    n_tiles = t_all // TILE
    ntb = n_tiles // n_batch
    tok, _, modr = _tile_maps(ntb, n_batch)
    if g_final is None:
        body, extra_in, extra_args = _combine_kernel, [], ()
        out_spec = pl.BlockSpec((TILE, d), tok)
        out_shape = jax.ShapeDtypeStruct((t_all, d), F32)
    else:
        body, extra_in, extra_args = _combine_final_kernel, [pl.BlockSpec((1, d), lambda i: (0, 0))], (g_final,)
        out_spec = pl.BlockSpec((None, TILE, d), lambda i: (i // ntb, jnp.maximum(i % ntb - 1, 0), 0))
        out_shape = jax.ShapeDtypeStruct((n_batch, (ntb - 1) * TILE, d), F32)
    return _combine_pallas(body, extra_in, extra_args, out_spec, out_shape, plan, sel, gate, x1, mod3, y, tok, modr)


def _combine_pallas(body, extra_in, extra_args, out_spec, out_shape, plan, sel, gate, x1, mod3, y, tok, modr):
    t_all, d = x1.shape
    n_tiles = t_all // TILE
    return pl.pallas_call(
        body,
        grid=(n_tiles,),
        in_specs=[pl.BlockSpec((1, 1, LANES), lambda i: (i, 0, 0), memory_space=pltpu.SMEM),
                  pl.BlockSpec((1, 1, LANES), lambda i: (jnp.minimum(i + 1, n_tiles - 1), 0, 0),
                               memory_space=pltpu.SMEM),
                  pl.BlockSpec((1, 1, LANES), lambda i: (jnp.minimum(i + COMBINE_AHEAD, n_tiles - 1), 0, 0),
                               memory_space=pltpu.SMEM),
                  pl.BlockSpec((TILE, LANES), tok), pl.BlockSpec((TILE, LANES), tok), pl.BlockSpec((TILE, d), tok),
                  pl.BlockSpec((1, 1, mod3.shape[-1]), modr)] + extra_in + [pl.BlockSpec(memory_space=pl.ANY)],
        out_specs=out_spec,
        out_shape=out_shape,
        scratch_shapes=[pltpu.VMEM((COMBINE_AHEAD + 1, SORT_W, d // 2), U32),
                        pltpu.SemaphoreType.DMA((COMBINE_AHEAD + 1,))],
        compiler_params=_cparams(("arbitrary",)),
        name="moe_combine",
    )(plan, plan, plan, sel, gate, x1, mod3, *extra_args, y)


def kernel(x, c, ctx, c_ctx, w_ada, b_ada, g_norm1, g_norm2, w_in, attn_sink, nat_rpb, mlstm_conv, mlstm_gate_bias, mla_g_q, mla_w_uq, mla_g_kv, mla_w_ukv, w_out, w_router, b_router, w_gu, b_gu, w_down, b_down, g_final):
    n_batch, s_len, d = x.shape
    ctx_len = ctx.shape[1]
    assert ctx_len == TILE and s_len % TILE == 0 and s_len // GRID_W >= NAT_KR
    depth = w_ada.shape[0]
    npb = ctx_len + s_len
    c_all = jnp.concatenate([c, c_ctx[None], jnp.zeros((8 - n_batch - 1, d), F32)], axis=0)
    mod_all = _mod_call(c_all, w_ada, b_ada)
    tabs = _rope_tables(s_len, ctx_len)
    x_all = jnp.concatenate([ctx, x], axis=1).reshape(n_batch * npb, d)
    n_le = depth * N_EXP
    ew = (_regroup_call(w_gu.reshape((n_le,) + w_gu.shape[2:])), _regroup_bias(b_gu.reshape(n_le, -1)),
          w_down.reshape((n_le,) + w_down.shape[2:]).astype(MXU), b_down.reshape(n_le, 1, d).astype(F32))
    for l in range(depth):
        pk = _pack_layer(w_in[l], mlstm_gate_bias[l], mla_w_uq[l], mla_w_ukv[l], mla_g_q[l], mla_g_kv[l],
                         w_out[l], mlstm_conv[l], nat_rpb[l])
        mod3 = mod_all[l].reshape(8, 1, 6 * d)
        (qa, ka, va, qb, kb, vb, qk, vc, oc, gi, gf, qd, kd, vd) = _proj_call(
            x_all, mod3, g_norm1[l].reshape(1, d), pk, tabs, n_batch)
        ya = _mixa_call(qa, ka, va, attn_sink[l], n_batch, s_len)
        yb = _mixb_call(qb, kb, vb, _nat_bias_table(pk["rpb"]), n_batch, s_len)
        hf, hb = _mixc_call(qk, vc, gi, gf, pk["conv"], n_batch)
        yd = _mixd_call(qd, kd, vd, n_batch)
        last = l == depth - 1
        x_all = _ffn_layer(x_all, mod3, ya, yb, hf, hb, oc, yd, pk["wo"], g_norm2[l], w_router[l], b_router[l],
                           ew, l * N_EXP, n_batch, g_final.reshape(1, d).astype(F32) if last else None)
    return x_all


def _ffn_layer(x_all, mod3, ya, yb, hf, hb, oc, yd, wo, g2, w_router, b_router, ew, exp_off, n_batch, g_final):
    t_all, d = x_all.shape
    wr = jnp.concatenate([w_router.astype(F32), jnp.zeros((d, LANES - N_EXP), F32)], axis=1)
    br = jnp.concatenate([b_router.astype(F32), jnp.full((LANES - N_EXP,), NEG, F32)]).reshape(1, LANES)
    x1, h2, sel, selt, gate, meta = _out_call(x_all, mod3, ya, yb, hf, hb, oc, yd, wo, g2.reshape(1, d),
                                              wr, br, n_batch)
    plan, tail, blk_exp, n_used, n_blk = _route_plan(meta, t_all)
    xs = _dispatch_call(plan, tail, selt, h2, n_blk * MOE_BLK)
    y = _expert_call(blk_exp + exp_off, n_used, xs, *ew)
    return _combine_call(plan, sel, gate, x1, mod3, y, n_batch, g_final)
```

```python
import functools

import jax
import jax.numpy as jnp
from jax import lax
from jax.experimental import pallas as pl
from jax.experimental.pallas import tpu as pltpu

F32 = jnp.float32
MXU = jnp.bfloat16

TILE = 256
PROJ_ROWS = 768
LANES = 128
HD = 64
GRID_W = 64
EPS = 1e-6
ROPE_BASE = 10000.0
NEG = -1e30
_LOG2E = 1.4426950408889634
D_VROWS = HD + 16

N_EXP = 32
TOP_K = 4
SWIGLU_LIMIT = 7.0
SWIGLU_ALPHA = 1.702
MOE_BLK = 256
RUN_ALIGN = 8

MLA_Q_LORA = 192
MLA_KV_LORA = 128
MLA_NOPE = 64
MLA_ROPE = 32
NAT_KR = 8
NAT_KC = 16

_IN_SIZES = (256, 128, 128, 256, 256, 256, 512, 256, 256, 16, 192, 128, 32)
_IN_OFF = tuple(sum(_IN_SIZES[:i]) for i in range(len(_IN_SIZES) + 1))

_C_QA, _C_QAR, _C_KA, _C_KAR, _C_VA = 0, 256, 512, 640, 768
_C_QB, _C_KB, _C_VB = 896, 1152, 1408
_C_QK, _C_VC, _C_OC, _C_GI, _C_GF = 1664, 2176, 2432, 2688, 2816
_C_CQ, _C_CKV, _C_KR, _C_KRR = 2944, 3200, 3328, 3456
_NC = 3584

_VMEM_LIMIT = 56 * 1024 * 1024


def _cparams(sem):
    return pltpu.CompilerParams(dimension_semantics=sem, vmem_limit_bytes=_VMEM_LIMIT)


def _dot(a, b):
    return jnp.dot(a, b, preferred_element_type=F32)


def _dot_nt(a, b):
    return lax.dot_general(a, b, (((1,), (1,)), ((), ())), preferred_element_type=F32)


def _split3(x):
    hi = x.astype(MXU)
    r1 = x - hi.astype(F32)
    mid = r1.astype(MXU)
    lo = (r1 - mid.astype(F32)).astype(MXU)
    return hi, mid, lo


def _head_cols(w, order, swap):
    d = w.shape[0]
    nh = w.shape[1] // HD
    w = jnp.stack([w.reshape(d, nh, HD)[:, i, :] for i in order], axis=1)
    if swap:
        w = jnp.concatenate([w[..., HD // 2:], w[..., :HD // 2]], axis=-1)
    return w.reshape(d, len(order) * HD)


def _pack_layer(w_in, gate_bias, w_uq, w_ukv, g_q, g_kv, w_out, conv_w, rpb):
    d = w_in.shape[0]
    o = _IN_OFF
    col = lambda i: w_in[:, o[i]:o[i + 1]]
    z = lambda n: jnp.zeros((d, n), F32)
    qa, ka, va, qb, kb, vb, qk, vc, oc, g, cq, ckv, kr = [col(i) for i in range(13)]
    a_ord = (0, 2, 1, 3)
    sc = HD ** -0.5
    gI = jnp.concatenate([g[:, 0:4], g[:, 8:12], z(120)], axis=1)
    gF = jnp.concatenate([g[:, 4:8], g[:, 12:16], z(120)], axis=1)
    hr = MLA_ROPE // 2
    kr_g = jnp.concatenate([z(64), kr, z(32)], axis=1)
    krr_g = jnp.concatenate([z(64), kr[:, hr:], kr[:, :hr], z(32)], axis=1)
    w1 = jnp.concatenate([
        _head_cols(qa, a_ord, False) * sc, _head_cols(qa, a_ord, True) * sc,
        ka, _head_cols(ka, (0, 1), True), va,
        qb * sc, kb, vb,
        qk, vc, oc, gI, gF,
        cq, z(64), ckv, kr_g, krr_g], axis=1)
    assert w1.shape[1] == _NC
    gb = gate_bias.astype(F32)
    gbias = jnp.stack([jnp.concatenate([gb[0], gb[2], jnp.zeros((120,), F32)]),
                       jnp.concatenate([gb[1], gb[3], jnp.zeros((120,), F32)])])
    dq = MLA_NOPE + MLA_ROPE
    sd = dq ** -0.5
    qh, qrh = [], []
    for h in range(4):
        nope = w_uq[:, h * dq:h * dq + MLA_NOPE]
        rope = w_uq[:, h * dq + MLA_NOPE:(h + 1) * dq]
        zq = lambda n: jnp.zeros((MLA_Q_LORA, n), F32)
        qh.append(jnp.concatenate([nope, rope, zq(32)], axis=1))
        qrh.append(jnp.concatenate([zq(64), rope[:, hr:], rope[:, :hr], zq(32)], axis=1))
    wuq = jnp.concatenate(qh + qrh, axis=1) * (sd * _LOG2E)
    wuq = jnp.concatenate([wuq, jnp.zeros((64, 1024), F32)], axis=0)
    kh = []
    for h in range(4):
        kn = w_ukv[:, h * 128:h * 128 + MLA_NOPE]
        kh.append(jnp.concatenate([kn, jnp.zeros((MLA_KV_LORA, 64), F32)], axis=1))
    vh = [w_ukv[:, h * 128 + MLA_NOPE:(h + 1) * 128] for h in range(4)]
    wukv = jnp.concatenate(kh + vh, axis=1)
    gq = jnp.concatenate([g_q.astype(F32), jnp.zeros((64,), F32)]).reshape(1, 256)
    gkv = g_kv.astype(F32).reshape(1, 128)
    wo = jnp.concatenate([w_out[i * HD:(i + 1) * HD] for i in a_ord] + [w_out[256:]], axis=0)
    return dict(w1=w1.astype(MXU), gbias=gbias, wuq=wuq.astype(MXU), wukv=wukv.astype(MXU),
                gq=gq, gkv=gkv, wo=wo.astype(MXU), conv=conv_w.astype(F32), rpb=rpb.astype(F32))


def _rope_tables(s_len, ctx_len):
    t = jnp.arange(s_len, dtype=jnp.int32)
    rows = (t // GRID_W).astype(F32)[:, None]
    cols = (t % GRID_W).astype(F32)[:, None]

    def ang(rot_dim):
        d_ax = rot_dim // 2
        inv = ROPE_BASE ** (-jnp.arange(0, d_ax, 2, dtype=F32) / d_ax)
        return jnp.concatenate([rows * inv, cols * inv], axis=-1)

    aa = ang(HD)
    ca, sa = jnp.cos(aa), jnp.sin(aa)
    cos_a = jnp.tile(jnp.concatenate([ca, ca], axis=-1), (1, 2))
    sin_a = jnp.tile(jnp.concatenate([-sa, sa], axis=-1), (1, 2))
    ad = ang(MLA_ROPE)
    cd, sd = jnp.cos(ad), jnp.sin(ad)
    one = lambda n: jnp.ones((s_len, n), F32)
    zero = lambda n: jnp.zeros((s_len, n), F32)
    cos_d = jnp.concatenate([one(64), cd, cd, one(32)], axis=-1)
    sin_d = jnp.concatenate([zero(64), -sd, sd, zero(32)], axis=-1)
    pad = lambda a, v: jnp.concatenate([jnp.full((ctx_len, LANES), v, F32), a], axis=0)
    return pad(cos_a, 1.0), pad(sin_a, 0.0), pad(cos_d, 1.0), pad(sin_d, 0.0)


def _mod_kernel(c_ref, w_ref, b_ref, o_ref):
    c = c_ref[...]
    s = c * jax.nn.sigmoid(c)
    w = w_ref[0]
    acc = jnp.zeros(o_ref.shape[1:], F32)
    sp = _split3(s)
    wp = _split3(w)
    for i, j in ((0, 0), (0, 1), (1, 0), (1, 1), (0, 2), (2, 0)):
        acc = acc + _dot(sp[i], wp[j])
    o_ref[0] = acc + b_ref[0]


def _mod_call(c_all, w_ada, b_ada):
    depth, d, n = w_ada.shape
    tn = 768
    return pl.pallas_call(
        _mod_kernel,
        grid=(depth, n // tn),
        in_specs=[pl.BlockSpec((8, d), lambda l, j: (0, 0)),
                  pl.BlockSpec((1, d, tn), lambda l, j: (l, 0, j)),
                  pl.BlockSpec((1, 1, tn), lambda l, j: (l, 0, j))],
        out_specs=pl.BlockSpec((1, 8, tn), lambda l, j: (l, 0, j)),
        out_shape=jax.ShapeDtypeStruct((depth, 8, n), F32),
        compiler_params=_cparams(("parallel", "parallel")),
        name="adaln_mod",
    )(c_all, w_ada, b_ada.reshape(depth, 1, n))


def _rms(x, n=None):
    n = x.shape[-1] if n is None else n
    return x * lax.rsqrt(jnp.sum(x * x, axis=-1, keepdims=True) * (1.0 / n) + EPS)


def _proj_kernel(x_ref, mod_ref, modc_ref, g1_ref, w1_ref, gb_ref, wuq_ref, wukv_ref, gq_ref, gkv_ref,
                 ca_ref, sa_ref, cd_ref, sd_ref,
                 qa_o, ka_o, va_o, qb_o, kb_o, vb_o, qk_o, vc_o, oc_o, gi_o, gf_o, qd_o, kd_o, vd_o, *, tiles_b):
    d = x_ref.shape[-1]
    x = x_ref[...]
    row = lax.broadcasted_iota(jnp.int32, (x.shape[0], 1), 0)
    is_ctx = (pl.program_id(0) % tiles_b == 0) & (row < TILE)
    mod = jnp.where(is_ctx, modc_ref[0][:, 0:2 * d], mod_ref[0][:, 0:2 * d])
    h = _rms(x) * g1_ref[...]
    h = (h * (1.0 + mod[:, d:2 * d]) + mod[:, 0:d]).astype(MXU)
    proj = lambda a, n: _dot(h, w1_ref[:, a:a + n])
    ca, sa, cd, sd = ca_ref[...], sa_ref[...], cd_ref[...], sd_ref[...]

    q, qr = proj(_C_QA, 256), proj(_C_QAR, 256)
    for g in range(2):
        sl = slice(g * LANES, (g + 1) * LANES)
        qa_o[:, sl] = (q[:, sl] * ca + qr[:, sl] * sa).astype(qa_o.dtype)
    ka_o[...] = (proj(_C_KA, 128) * ca + proj(_C_KAR, 128) * sa).astype(ka_o.dtype)
    va_o[...] = proj(_C_VA, 128).astype(va_o.dtype)
    qb_o[...] = proj(_C_QB, 256).astype(qb_o.dtype)
    kb_o[...] = proj(_C_KB, 256).astype(kb_o.dtype)
    vb_o[...] = proj(_C_VB, 256).astype(vb_o.dtype)
    qk_o[...] = proj(_C_QK, 512).astype(qk_o.dtype)
    vc_o[...] = proj(_C_VC, 256).astype(vc_o.dtype)
    oc_o[...] = proj(_C_OC, 256).astype(oc_o.dtype)
    gb = gb_ref[...]
    gi_o[...] = proj(_C_GI, 128) + gb[0:1, :]
    gf_o[...] = proj(_C_GF, 128) + gb[1:2, :]
    cq = proj(_C_CQ, 256)
    cqn = (_rms(cq, MLA_Q_LORA) * gq_ref[...]).astype(MXU)
    q2 = _dot(cqn, wuq_ref[...])
    ckv = proj(_C_CKV, 128)
    ckvn = (_rms(ckv) * gkv_ref[...]).astype(MXU)
    kv2 = _dot(ckvn, wukv_ref[...])
    krope = proj(_C_KR, 128) * cd + proj(_C_KRR, 128) * sd
    for hh in range(4):
        sl = slice(hh * LANES, (hh + 1) * LANES)
        sr = slice(512 + hh * LANES, 512 + (hh + 1) * LANES)
        qd_o[:, sl] = (q2[:, sl] * cd + q2[:, sr] * sd).astype(qd_o.dtype)
        kd_o[:, sl] = (kv2[:, sl] + krope).astype(kd_o.dtype)
    vt = kv2[:, 512:768].T
    ones = jnp.ones((D_VROWS - HD, vt.shape[1]), F32)
    for hh in range(4):
        vd_o[hh * D_VROWS:(hh + 1) * D_VROWS, :] = jnp.concatenate(
            [vt[hh * HD:(hh + 1) * HD, :], ones], axis=0).astype(vd_o.dtype)


def _tile_maps(n_tiles_b, n_batch):
    tok = lambda i: (i, 0)
    pos = lambda i: (i % n_tiles_b, 0)
    modr = lambda i: (jnp.where(i % n_tiles_b == 0, n_batch, i // n_tiles_b), 0, 0)
    return tok, pos, modr


def _proj_call(x_all, mod3, g1, pk, tabs, n_batch):
    t_all, d = x_all.shape
    npb = t_all // n_batch
    rt = PROJ_ROWS if npb % PROJ_ROWS == 0 else TILE
    n_tiles = t_all // rt
    ntb = n_tiles // n_batch
    tok = lambda i: (i, 0)
    pos = lambda i: (i % ntb, 0)
    const = lambda i: (0, 0)
    full = lambda a: pl.BlockSpec(a.shape, const)
    widths = (256, 128, 128, 256, 256, 256, 512, 256, 256, 128, 128, 512, 512)
    dts = (MXU,) * 9 + (F32, F32) + (MXU,) * 2
    vt_spec = pl.BlockSpec((None, 4 * D_VROWS, rt), lambda i: (i // ntb, 0, i % ntb))
    vt_shape = jax.ShapeDtypeStruct((n_batch, 4 * D_VROWS, npb), MXU)
    tab = pl.BlockSpec((rt, LANES), pos)
    modw = mod3.shape[-1]
    return pl.pallas_call(
        functools.partial(_proj_kernel, tiles_b=ntb),
        grid=(n_tiles,),
        in_specs=[pl.BlockSpec((rt, d), tok), pl.BlockSpec((1, 1, modw), lambda i: (i // ntb, 0, 0)),
                  pl.BlockSpec((1, 1, modw), lambda i: (n_batch, 0, 0)), full(g1),
                  full(pk["w1"]), full(pk["gbias"]), full(pk["wuq"]), full(pk["wukv"]),
                  full(pk["gq"]), full(pk["gkv"]), tab, tab, tab, tab],
        out_specs=[pl.BlockSpec((rt, w), tok) for w in widths] + [vt_spec],
        out_shape=[jax.ShapeDtypeStruct((t_all, w), dt) for w, dt in zip(widths, dts)] + [vt_shape],
        compiler_params=_cparams(("parallel",)),
        name="in_proj",
    )(x_all, mod3, mod3, g1, pk["w1"], pk["gbias"], pk["wuq"], pk["wukv"], pk["gq"], pk["gkv"], *tabs)


_A_ORD = (0, 2, 1, 3)
A_BLK = 128
A_WIN = 128


def _mixa_kernel(sink_ref, q_ref, kp_ref, kc_ref, kn_ref, kx_ref, vp_ref, vc_ref, vn_ref, vx_ref, o_ref,
                 *, s_len, ctx_blocks):
    n = pl.program_id(1)
    is_lat = n >= ctx_blocks
    nb = 3 * A_BLK
    lane = lax.broadcasted_iota(jnp.int32, (A_BLK, LANES), 1)
    lo = lane < HD
    qi = lax.broadcasted_iota(jnp.int32, (2 * A_BLK, nb), 0) & (A_BLK - 1)
    kj = lax.broadcasted_iota(jnp.int32, (2 * A_BLK, nb), 1)
    qpos = (n - ctx_blocks) * A_BLK + qi
    kpos = (n - ctx_blocks - 1) * A_BLK + kj
    ok = is_lat & (kpos >= 0) & (kpos < s_len) & (jnp.abs(qpos - kpos) <= A_WIN)
    row = lax.broadcasted_iota(jnp.int32, (2 * A_BLK, 1), 0)
    k_all = jnp.concatenate([kp_ref[...], kc_ref[...], kn_ref[...], kx_ref[...]], axis=0)
    v_all = jnp.concatenate([vp_ref[...], vc_ref[...], vn_ref[...], vx_ref[...]], axis=0)
    q = q_ref[...]
    zero = jnp.zeros((A_BLK, LANES), q.dtype)
    scores = []
    for g in range(2):
        qg = q[:, g * LANES:(g + 1) * LANES]
        q2 = jnp.concatenate([jnp.where(lo, qg, zero), jnp.where(lo, zero, qg)], axis=0)
        scores.append(_dot_nt(q2, k_all))
    for g in range(2):
        s = scores[g]
        s = jnp.concatenate([jnp.where(ok, s[:, :nb], NEG), s[:, nb:]], axis=1)
        sink = jnp.where(row < A_BLK, sink_ref[_A_ORD[2 * g]], sink_ref[_A_ORD[2 * g + 1]])
        m = jnp.maximum(jnp.max(s, axis=-1, keepdims=True), sink)
        p = jnp.exp(s - m)
        den = jnp.sum(p, axis=-1, keepdims=True) + jnp.exp(sink - m)
        pv = _dot(p.astype(MXU), v_all) / den
        o_ref[:, g * LANES:(g + 1) * LANES] = jnp.where(lo, pv[:A_BLK], pv[A_BLK:]).astype(o_ref.dtype)


def _mixa_call(qa, ka, va, sink, n_batch, s_len):
    t_all = qa.shape[0]
    nq = t_all // n_batch // A_BLK
    cb = TILE // A_BLK
    qmap = lambda b, n: (b * nq + n, 0)
    pmap = lambda b, n: (b * nq + jnp.clip(n - 1, cb, nq - 1), 0)
    nmap = lambda b, n: (b * nq + jnp.clip(n + 1, cb, nq - 1), 0)
    xmap = lambda b, n: (b * (nq // cb), 0)
    kv = lambda m: pl.BlockSpec((A_BLK, LANES), m)
    kvx = pl.BlockSpec((TILE, LANES), xmap)
    return pl.pallas_call(
        functools.partial(_mixa_kernel, s_len=s_len, ctx_blocks=cb),
        grid=(n_batch, nq),
        in_specs=[pl.BlockSpec(memory_space=pltpu.SMEM), pl.BlockSpec((A_BLK, 256), qmap),
                  kv(pmap), kv(qmap), kv(nmap), kvx, kv(pmap), kv(qmap), kv(nmap), kvx],
        out_specs=pl.BlockSpec((A_BLK, 256), qmap),
        out_shape=jax.ShapeDtypeStruct((t_all, 256), MXU),
        compiler_params=_cparams(("parallel", "parallel")),
        name="mixer_a",
    )(sink.astype(F32), qa, ka, ka, ka, ka, va, va, va, va)


def _nat_bias_table(rpb):
    qc = jnp.arange(GRID_W)[:, None]
    kc = jnp.arange(GRID_W)[None, :]
    cs = jnp.clip(qc - NAT_KC // 2, 0, GRID_W - NAT_KC)
    ok = (kc >= cs) & (kc < cs + NAT_KC)
    dc = jnp.clip(kc - qc + NAT_KC - 1, 0, 2 * NAT_KC - 2)
    t = jnp.where(ok[None, None], rpb[:, :, dc], NEG)
    pair = lambda h: jnp.concatenate([t[h, :-1], t[h, 1:]], axis=-1)
    return jnp.stack([jnp.concatenate([pair(2 * g), pair(2 * g + 1)], axis=1) for g in range(2)])


def _mixb_kernel(q_ref, k_ref, v_ref, tb_ref, o_ref, *, rows):
    j = pl.program_id(1)
    is_ctx = j == 0
    nloc = NAT_KR * GRID_W
    lane = lax.broadcasted_iota(jnp.int32, (GRID_W, LANES), 1)
    lo = lane < HD
    zero = jnp.zeros((GRID_W, LANES), q_ref.dtype)
    for i in range(TILE // GRID_W):
        r = jnp.maximum((j - 1) * (TILE // GRID_W) + i, 0)
        r0 = jnp.clip(r - NAT_KR // 2, 0, rows - NAT_KR)
        kstart = pl.multiple_of(TILE + r0 * GRID_W, GRID_W)
        dr0 = r0 - r + NAT_KR - 1
        for g in range(2):
            sl = slice(g * LANES, (g + 1) * LANES)
            qg = q_ref[i * GRID_W:(i + 1) * GRID_W, sl]
            q2 = jnp.concatenate([jnp.where(lo, qg, zero), jnp.where(lo, zero, qg)], axis=0)
            s_loc = _dot_nt(q2, k_ref[pl.ds(kstart, nloc), sl])
            s_ctx = _dot_nt(q2, k_ref[0:TILE, sl])
            bias = jnp.concatenate([tb_ref[g, dr0 + 2 * t] for t in range(NAT_KR // 2)], axis=1)
            s_loc = jnp.where(is_ctx, NEG, s_loc + bias)
            m = jnp.maximum(jnp.max(s_loc, axis=-1, keepdims=True), jnp.max(s_ctx, axis=-1, keepdims=True))
            p_loc = jnp.exp(s_loc - m)
            p_ctx = jnp.exp(s_ctx - m)
            den = jnp.sum(p_loc, axis=-1, keepdims=True) + jnp.sum(p_ctx, axis=-1, keepdims=True)
            pv = _dot(p_loc.astype(MXU), v_ref[pl.ds(kstart, nloc), sl]) + _dot(p_ctx.astype(MXU), v_ref[0:TILE, sl])
            pv = pv / den
            o_ref[i * GRID_W:(i + 1) * GRID_W, sl] = jnp.where(lo, pv[:GRID_W], pv[GRID_W:]).astype(o_ref.dtype)


def _mixb_call(qb, kb, vb, tb, n_batch, s_len):
    t_all = qb.shape[0]
    npb = t_all // n_batch
    ntb = npb // TILE
    qmap = lambda b, j: (b * ntb + j, 0)
    kvmap = lambda b, j: (b, 0, 0)
    kvspec = pl.BlockSpec((None, npb, 256), kvmap)
    return pl.pallas_call(
        functools.partial(_mixb_kernel, rows=s_len // GRID_W),
        grid=(n_batch, ntb),
        in_specs=[pl.BlockSpec((TILE, 256), qmap), kvspec, kvspec,
                  pl.BlockSpec(tb.shape, lambda b, j: (0, 0, 0, 0))],
        out_specs=pl.BlockSpec((TILE, 256), qmap),
        out_shape=jax.ShapeDtypeStruct((t_all, 256), MXU),
        compiler_params=_cparams(("parallel", "arbitrary")),
        name="mixer_b",
    )(qb, kb.reshape(n_batch, npb, 256), vb.reshape(n_batch, npb, 256), tb)


D_TK = 256
D_UNROLL = 16


def _mixd_kernel(q_ref, k_ref, vt_ref, o_ref, acc_sc):
    j = pl.program_id(1)
    n_lat = (pl.num_programs(1) - 1) * TILE
    n_rest = jnp.where(j == 0, 0, n_lat // (D_TK * D_UNROLL))
    acc_sc[...] = jnp.zeros(acc_sc.shape, F32)
    qs = [q_ref[:, h * LANES:(h + 1) * LANES] for h in range(4)]

    def steps(starts, tk, ms):
        ms = list(ms)
        sts = [[_dot_nt(k_ref[pl.ds(ks, tk), h * LANES:(h + 1) * LANES], qs[h]) for h in range(4)]
               for ks in starts]
        for ks, st4 in zip(starts, sts):
            for h in range(4):
                st = st4[h].astype(MXU)
                m_new = jnp.maximum(ms[h], jnp.max(st, axis=0, keepdims=True).astype(F32))
                alpha = jnp.exp2(ms[h] - m_new)
                pt = jnp.exp2(st - m_new.astype(MXU))
                ms[h] = m_new
                pv = _dot(vt_ref[h * D_VROWS:(h + 1) * D_VROWS, pl.ds(ks, tk)], pt)
                acc_sc[h] = acc_sc[h] * alpha + pv
        return tuple(ms)

    init = tuple(jnp.full((1, TILE), NEG, F32) for _ in range(4))
    carry = steps([0], TILE, init)
    body = lambda i, c: steps([pl.multiple_of(TILE + (D_UNROLL * i + u) * D_TK, TILE) for u in range(D_UNROLL)],
                              D_TK, c)
    lax.fori_loop(0, n_rest, body, carry)
    for p in range(2):
        outs = [acc_sc[h, 0:HD, :] / acc_sc[h, HD:HD + 1, :] for h in (2 * p, 2 * p + 1)]
        o_ref[:, p * LANES:(p + 1) * LANES] = jnp.concatenate(outs, axis=0).T.astype(o_ref.dtype)


def _mixd_call(qd, kd, vdt, n_batch):
    t_all = qd.shape[0]
    npb = t_all // n_batch
    ntb = npb // TILE
    assert ((ntb - 1) * TILE) % (D_TK * D_UNROLL) == 0
    qmap = lambda b, j: (b * ntb + j, 0)
    kvmap = lambda b, j: (b, 0, 0)
    return pl.pallas_call(
        _mixd_kernel,
        grid=(n_batch, ntb),
        in_specs=[pl.BlockSpec((TILE, 512), qmap), pl.BlockSpec((None, npb, 512), kvmap),
                  pl.BlockSpec((None, 4 * D_VROWS, npb), kvmap)],
        out_specs=pl.BlockSpec((TILE, 256), qmap),
        out_shape=jax.ShapeDtypeStruct((t_all, 256), MXU),
        scratch_shapes=[pltpu.VMEM((4, D_VROWS, TILE), F32)],
        compiler_params=_cparams(("parallel", "arbitrary")),
        name="mixer_d",
    )(qd, kd.reshape(n_batch, npb, 512), vdt)


C_L = 128
HALO = 16


def _log_sigmoid(x):
    return jnp.minimum(x, 0.0) - jnp.log1p(jnp.exp(-jnp.abs(x)))


def _mlstm_prep(d, j, cb, nch, qk_ref, hp_ref, hn_ref, v_ref, gi_ref, gf_ref, conv_ref, c_sc, n_sc):
    row = lax.broadcasted_iota(jnp.int32, (C_L, 1), 0)
    rr = lax.broadcasted_iota(jnp.int32, (C_L, C_L), 0)
    cc = lax.broadcasted_iota(jnp.int32, (C_L, C_L), 1)
    lo = cc < HD
    causal = (cc <= rr) if d == 0 else (cc >= rr)

    prev_ok = (j != 0) & (j != cb)
    next_ok = (j != cb - 1) & (j != nch - 1)
    x = qk_ref[...].astype(F32)
    prow = jnp.where(prev_ok, hp_ref[HALO - 1:HALO, :].astype(F32), 0.0)
    nrow = jnp.where(next_ok, hn_ref[0:1, :].astype(F32), 0.0)
    xm1 = jnp.where(row == 0, prow, pltpu.roll(x, 1, 0))
    xp1 = jnp.where(row == C_L - 1, nrow, pltpu.roll(x, C_L - 1, 0))
    w = conv_ref[...]
    u = xm1 * w[0:1, :] + x * w[1:2, :] + xp1 * w[2:3, :]
    a = u * jax.nn.sigmoid(u)
    q_all = a[:, 0:256].astype(MXU)
    k_all = a[:, 256:512] * (HD ** -0.5)

    f = _log_sigmoid(gf_ref[...])
    bc = sum(_dot(causal.astype(MXU), part) for part in _split3(f))
    g = gi_ref[...] - bc
    st = dict(d=d, row=row, lo=lo, causal=causal, blockdiag=(rr < HD) == (cc < HD), bc=bc, g=g, gt=g.T,
              last=C_L - 1 if d == 0 else 0, units=[], pairs=[])
    for p in range(2):
        sl = slice(p * LANES, (p + 1) * LANES)
        qp = q_all[:, sl]
        kp32 = k_all[:, sl]
        kp = kp32.astype(MXU)
        cmat, nmat = c_sc[d, p], n_sc[d, p]
        c16, n16 = cmat.astype(MXU), nmat.astype(MXU)
        zero = jnp.zeros_like(qp)
        st["pairs"].append(dict(sl=sl, kp32=kp32, vp=v_ref[:, sl], cmat=cmat, nmat=nmat))
        for hh in range(2):
            qh = jnp.where(lo, qp, zero) if hh == 0 else jnp.where(lo, zero, qp)
            st["units"].append(dict(ln=d * 4 + 2 * p + hh, s=_dot_nt(qh, kp), qc=_dot(qh, c16), qn=_dot(qh, n16)))
    return st


def _mlstm_gates(st, m_sc):
    causal, bc, g, gt, last = st["causal"], st["bc"], st["g"], st["gt"], st["last"]
    for un in st["units"]:
        ln = un["ln"]
        bcol = bc[:, ln:ln + 1]
        logd = jnp.where(causal, bcol + gt[ln:ln + 1, :], NEG)
        mloc = jnp.max(logd, axis=-1, keepdims=True)
        m_prev = m_sc[ln:ln + 1, 0:1]
        m_t = jnp.maximum(bcol + m_prev, mloc)
        sw = un["s"] * jnp.exp(logd - m_t)
        a_end = mloc[last:last + 1, :]
        b_end = bcol[last:last + 1, :]
        m_new = jnp.maximum(b_end + m_prev, a_end)
        un.update(sw=sw, m_t=m_t, inter=jnp.exp(bcol + m_prev - m_t), m_new=m_new,
                  decay=jnp.exp(b_end + m_prev - m_new), scl=jnp.exp(a_end - m_new),
                  wcol=jnp.exp(g[:, ln:ln + 1] + (b_end - a_end)))


def _mlstm_finish(st, o_ref, c_sc, n_sc, m_sc):
    d, lo, row = st["d"], st["lo"], st["row"]
    for p, pr in enumerate(st["pairs"]):
        us = st["units"][2 * p:2 * p + 2]
        nums = [_dot(un["sw"].astype(MXU), pr["vp"]) for un in us]
        kwt = (pr["kp32"] * jnp.where(lo, us[0]["wcol"], us[1]["wcol"])).T.astype(MXU)
        kv = jnp.where(st["blockdiag"], _dot(kwt, pr["vp"]), 0.0)
        ksum = _dot(kwt, jnp.ones((C_L, LANES), MXU))
        houts = []
        for un, num in zip(us, nums):
            den = jnp.sum(un["sw"], axis=-1, keepdims=True) + un["inter"] * un["qn"]
            houts.append((num + un["inter"] * un["qc"]) / jnp.maximum(jnp.abs(den), jnp.exp(-un["m_t"])))
            m_sc[un["ln"]:un["ln"] + 1, :] = jnp.broadcast_to(un["m_new"], (1, LANES))
        o_ref[:, pr["sl"]] = jnp.where(lo, houts[0], houts[1])
        dec = jnp.where(row < HD, us[0]["decay"], us[1]["decay"])
        scl = jnp.where(row < HD, us[0]["scl"], us[1]["scl"])
        c_sc[d, p] = dec * pr["cmat"] + scl * kv
        n_sc[d, p] = dec * pr["nmat"] + scl * ksum


def _mixc_kernel(qkf, hpf, hnf, vf, gif, gff, qkb, hpb, hnb, vb, gib, gfb, conv_ref, of_ref, ob_ref,
                 c_sc, n_sc, m_sc, *, cb, nch):
    i = pl.program_id(1)

    @pl.when(i == 0)
    def _():
        c_sc[...] = jnp.zeros(c_sc.shape, F32)
        n_sc[...] = jnp.zeros(n_sc.shape, F32)
        m_sc[...] = jnp.zeros(m_sc.shape, F32)

    jb = jnp.where(i < cb, cb - 1 - i, nch + cb - 1 - i)
    sf = _mlstm_prep(0, i, cb, nch, qkf, hpf, hnf, vf, gif, gff, conv_ref, c_sc, n_sc)
    sb = _mlstm_prep(1, jb, cb, nch, qkb, hpb, hnb, vb, gib, gfb, conv_ref, c_sc, n_sc)
    _mlstm_gates(sf, m_sc)
    _mlstm_gates(sb, m_sc)
    _mlstm_finish(sf, of_ref, c_sc, n_sc, m_sc)
    _mlstm_finish(sb, ob_ref, c_sc, n_sc, m_sc)


def _mixc_call(qk, vc, gi, gf, conv_w, n_batch):
    t_all = qk.shape[0]
    nch = t_all // n_batch // C_L
    cb = TILE // C_L
    hb = C_L // HALO
    n_halo = t_all // HALO
    jf = lambda i: i
    jb = lambda i: jnp.where(i < cb, cb - 1 - i, nch + cb - 1 - i)

    def specs(jmap):
        cur = lambda b, i: (b * nch + jmap(i), 0)
        prv = lambda b, i: (jnp.maximum((b * nch + jmap(i)) * hb - 1, 0), 0)
        nxt = lambda b, i: (jnp.minimum((b * nch + jmap(i) + 1) * hb, n_halo - 1), 0)
        return cur, [pl.BlockSpec((C_L, 512), cur), pl.BlockSpec((HALO, 512), prv), pl.BlockSpec((HALO, 512), nxt),
                     pl.BlockSpec((C_L, 256), cur), pl.BlockSpec((C_L, LANES), cur), pl.BlockSpec((C_L, LANES), cur)]

    cur_f, in_f = specs(jf)
    cur_b, in_b = specs(jb)
    args = (qk, qk, qk, vc, gi, gf)
    return pl.pallas_call(
        functools.partial(_mixc_kernel, cb=cb, nch=nch),
        grid=(n_batch, nch),
        in_specs=in_f + in_b + [pl.BlockSpec(conv_w.shape, lambda b, i: (0, 0))],
        out_specs=[pl.BlockSpec((C_L, 256), cur_f), pl.BlockSpec((C_L, 256), cur_b)],
        out_shape=[jax.ShapeDtypeStruct((t_all, 256), F32)] * 2,
        scratch_shapes=[pltpu.VMEM((2, 2, LANES, LANES), F32), pltpu.VMEM((2, 2, LANES, LANES), F32),
                        pltpu.VMEM((8, LANES), F32)],
        compiler_params=_cparams(("parallel", "arbitrary")),
        name="mixer_c",
    )(*args, *args, conv_w)


def _out_kernel(x_ref, mod_ref, ya_ref, yb_ref, hf_ref, hb_ref, oc_ref, yd_ref, wo_ref, g2_ref, wr_ref, br_ref,
                x1_o, h2_o, sel_o, selt_o, gate_o, meta_o, carry_sc):
    d = x_ref.shape[-1]
    i = pl.program_id(0)

    @pl.when(i == 0)
    def _():
        carry_sc[...] = jnp.zeros(carry_sc.shape, F32)

    mod = mod_ref[0]
    ym = (jax.nn.sigmoid(oc_ref[...].astype(F32)) * (hf_ref[...] + hb_ref[...])).astype(MXU)
    acc = (_dot(ya_ref[...], wo_ref[0:256, :]) + _dot(yb_ref[...], wo_ref[256:512, :])
           + _dot(ym, wo_ref[512:768, :]) + _dot(yd_ref[...], wo_ref[768:1024, :]))
    x1 = x_ref[...] + mod[:, 2 * d:3 * d] * acc
    x1_o[...] = x1
    h2 = _rms(x1) * g2_ref[...]
    h2 = h2 * (1.0 + mod[:, 4 * d:5 * d]) + mod[:, 3 * d:4 * d]
    h2_o[...] = h2.astype(h2_o.dtype)

    hp = _split3(h2)
    wp = _split3(wr_ref[...])
    logits = br_ref[...]
    for a, b in ((0, 0), (0, 1), (1, 0), (1, 1), (0, 2), (2, 0)):
        logits = logits + _dot(hp[a], wp[b])
    lane = lax.broadcasted_iota(jnp.int32, (TILE, LANES), 1)
    vals, idxs = [], []
    cur = logits
    for _ in range(TOP_K):
        mk = jnp.max(cur, axis=-1, keepdims=True)
        ik = jnp.min(jnp.where(cur == mk, lane, LANES), axis=-1, keepdims=True)
        cur = jnp.where(lane == ik, NEG, cur)
        vals.append(mk)
        idxs.append(ik)
    es = [jnp.exp(v - vals[0]) for v in vals]
    esum = es[0] + es[1] + es[2] + es[3]
    hot = [(lane == ik) for ik in idxs]
    multi = sum(h.astype(F32) for h in hot)
    rr = lax.broadcasted_iota(jnp.int32, (TILE, TILE), 0)
    cc = lax.broadcasted_iota(jnp.int32, (TILE, TILE), 1)
    before = _dot((cc < rr).astype(MXU), multi.astype(MXU))
    tile_cnt = jnp.sum(multi, axis=0, keepdims=True)
    sel_t = jnp.zeros((TILE, LANES), F32)
    gate_t = jnp.zeros((TILE, LANES), F32)
    for k in range(TOP_K):
        lr = jnp.sum(jnp.where(hot[k], before, 0.0), axis=-1, keepdims=True)
        sel_t = jnp.where(lane == k, idxs[k].astype(F32), sel_t)
        sel_t = jnp.where(lane == TOP_K + k, lr, sel_t)
        gate_t = jnp.where(lane == k, es[k] / esum, gate_t)
    sel_o[...] = sel_t.astype(jnp.int32)
    selt_o[0] = sel_t.T[0:8, :].astype(jnp.int32)
    gate_o[...] = gate_t
    row8 = lax.broadcasted_iota(jnp.int32, (8, LANES), 0)
    meta = jnp.where(row8 == 0, carry_sc[...], jnp.where(row8 == 1, tile_cnt, 0.0))
    meta_o[0] = meta.astype(jnp.int32)
    carry_sc[...] = carry_sc[...] + jnp.ceil(tile_cnt * (1.0 / RUN_ALIGN)) * RUN_ALIGN


def _out_call(x_all, mod3, ya, yb, hf, hb, oc, yd, wo, g2, wr, br, n_batch):
    t_all, d = x_all.shape
    n_tiles = t_all // TILE
    tok, _, modr = _tile_maps(n_tiles // n_batch, n_batch)
    const = lambda i: (0, 0)
    full = lambda a: pl.BlockSpec(a.shape, const)
    t256 = pl.BlockSpec((TILE, 256), tok)
    td = pl.BlockSpec((TILE, d), tok)
    tl = pl.BlockSpec((TILE, LANES), tok)
    return pl.pallas_call(
        _out_kernel,
        grid=(n_tiles,),
        in_specs=[td, pl.BlockSpec((1, 1, mod3.shape[-1]), modr), t256, t256, t256, t256, t256, t256,
                  full(wo), full(g2), full(wr), full(br)],
        out_specs=[td, td, tl, pl.BlockSpec((1, 8, TILE), lambda i: (i, 0, 0)), tl,
                   pl.BlockSpec((1, 8, LANES), lambda i: (i, 0, 0))],
        out_shape=[jax.ShapeDtypeStruct((t_all, d), F32), jax.ShapeDtypeStruct((t_all, d), MXU),
                   jax.ShapeDtypeStruct((t_all, LANES), jnp.int32), jax.ShapeDtypeStruct((n_tiles, 8, TILE), jnp.int32),
                   jax.ShapeDtypeStruct((t_all, LANES), F32), jax.ShapeDtypeStruct((n_tiles, 8, LANES), jnp.int32)],
        scratch_shapes=[pltpu.VMEM((1, LANES), F32)],
        compiler_params=_cparams(("arbitrary",)),
        name="out_proj_router",
    )(x_all, mod3, ya, yb, hf, hb, oc, yd, wo, g2, wr, br)


SORT_R = 48
SORT_W = N_EXP * SORT_R
_M_CNT, _M_ROUNDS = N_EXP, 2 * N_EXP


def _route_plan(meta, t_all):
    base = meta[:, 0, :N_EXP]
    cnt = meta[:, 1, :N_EXP]
    total = base[-1] + (cnt[-1] + RUN_ALIGN - 1) // RUN_ALIGN * RUN_ALIGN
    padded = (total + SORT_R + MOE_BLK - 1) // MOE_BLK * MOE_BLK
    pad_end = jnp.cumsum(padded)
    pad_start = pad_end - padded
    rounds = jnp.maximum((jnp.max(cnt, axis=1, keepdims=True) + SORT_R - 1) // SORT_R, 1)
    plan = jnp.concatenate([pad_start[None, :] + base, cnt, rounds,
                            jnp.zeros((meta.shape[0], LANES - 2 * N_EXP - 1), jnp.int32)], axis=1)
    n_runs = meta.shape[0] * N_EXP
    n_blk = (t_all * TOP_K + n_runs * (RUN_ALIGN - 1) + N_EXP * (SORT_R + MOE_BLK - 1) + MOE_BLK - 1) // MOE_BLK
    blk_row = jnp.arange(n_blk, dtype=jnp.int32)[:, None] * MOE_BLK
    blk_exp = jnp.minimum(jnp.sum((blk_row >= pad_end[None, :]).astype(jnp.int32), axis=1), N_EXP - 1)
    n_used = (pad_end[-1] // MOE_BLK).astype(jnp.int32).reshape(1)
    last_rounds = jnp.maximum((cnt[-1] + SORT_R - 1) // SORT_R, 1)
    tail_start = pad_start + base[-1] + SORT_R * last_rounds
    tail = jnp.concatenate([tail_start, (pad_end - tail_start) // TAIL_ROWS,
                            pad_end[-1:], n_blk - pad_end[-1:] // MOE_BLK,
                            jnp.zeros((LANES - 2 * N_EXP - 2,), jnp.int32)]).astype(jnp.int32).reshape(1, 1, LANES)
    return plan.astype(jnp.int32)[:, None, :], tail, blk_exp.astype(jnp.int32), n_used, n_blk


U32 = jnp.uint32
_HI16 = 0xFFFF0000


def _pack_pairs(x):
    half = x.shape[1] // 2
    bits = lax.bitcast_convert_type(x, U32)
    return (bits[:, half:] & U32(_HI16)) | (bits[:, :half] >> 16)


def _unpack_pairs(w):
    lo = lax.bitcast_convert_type(w << 16, F32).astype(MXU)
    hi = lax.bitcast_convert_type(w & U32(_HI16), F32).astype(MXU)
    return jnp.concatenate([lo, hi], axis=1)


def _run_start(plan_ref, e, r):
    return pl.multiple_of(plan_ref[0, 0, e] + r * SORT_R, RUN_ALIGN)


def _extra_round_copies(plan_ref, r, make_copy):
    for go in (lambda c: c.start(), lambda c: c.wait()):
        for e in range(N_EXP):
            @pl.when(plan_ref[0, 0, _M_CNT + e] > r * SORT_R)
            def _():
                go(make_copy(e, _run_start(plan_ref, e, r)))


def _round_cols(sel_e, sel_lr, r):
    lo = r * SORT_R
    ok = (sel_lr >= lo) & (sel_lr < lo + SORT_R)
    return jnp.where(ok, sel_e * SORT_R + sel_lr - lo, -1)


TAIL_ROWS = 8


def _dispatch_kernel(plan_ref, tail_ref, selt_ref, h2_ref, xs_o, stage, zrows, sem, zsem, pending):
    i = pl.program_id(0)
    last = pl.num_programs(0) - 1
    slot = i % 2
    h2 = h2_ref[...]
    rows = lax.broadcasted_iota(jnp.int32, (TILE, TILE), 0)

    def run_copy(buf, e, dst):
        return pltpu.make_async_copy(stage.at[buf, pl.ds(e * SORT_R, SORT_R), :],
                                     xs_o.at[pl.ds(dst, SORT_R), :], sem.at[buf, e])

    def wait_round0(buf):
        for e in range(N_EXP):
            run_copy(buf, e, 0).wait()

    def sort_round(r):
        cols = [_round_cols(selt_ref[0, k:k + 1, :], selt_ref[0, TOP_K + k:TOP_K + k + 1, :], r)
                for k in range(TOP_K)]
        for c in range(SORT_W // TILE):
            hit = jnp.zeros((TILE, TILE), F32)
            for k in range(TOP_K):
                hit = jnp.where(rows + c * TILE == cols[k], 1.0, hit)
            stage[slot, c * TILE:(c + 1) * TILE, :] = _pack_pairs(_dot(hit.astype(MXU), h2))

    @pl.when(i == 0)
    def _():
        pending[0] = 0

    sort_round(0)

    def start_all(after_previous):
        for e in range(N_EXP):
            if after_previous:
                run_copy(1 - slot, e, 0).wait()
            run_copy(slot, e, _run_start(plan_ref, e, 0)).start(priority=e % 2)

    @pl.when(pending[0] == 1)
    def _():
        start_all(True)

    @pl.when(pending[0] == 0)
    def _():
        start_all(False)

    pending[0] = 1
    rounds = plan_ref[0, 0, _M_ROUNDS]

    @pl.when((rounds > 1) | (i == last))
    def _():
        wait_round0(slot)
        pending[0] = 0

    def extra_round(r, carry):
        sort_round(r)
        _extra_round_copies(plan_ref, r, lambda e, dst: run_copy(slot, e, dst))
        return carry

    lax.fori_loop(1, rounds, extra_round, 0)

    @pl.when(i == last)
    def _():
        zrows[...] = jnp.zeros(zrows.shape, zrows.dtype)
        for go in (lambda c: c.start(), lambda c: c.wait()):
            for e in range(N_EXP):
                def fill(t, carry, e=e):
                    dst = pl.multiple_of(tail_ref[0, 0, e] + t * TAIL_ROWS, TAIL_ROWS)
                    go(pltpu.make_async_copy(zrows.at[pl.ds(0, TAIL_ROWS), :], xs_o.at[pl.ds(dst, TAIL_ROWS), :], zsem))
                    return carry
                lax.fori_loop(0, tail_ref[0, 0, _M_CNT + e], fill, 0)

            def fill_block(t, carry):
                dst = pl.multiple_of(tail_ref[0, 0, _M_ROUNDS] + t * MOE_BLK, MOE_BLK)
                go(pltpu.make_async_copy(zrows, xs_o.at[pl.ds(dst, MOE_BLK), :], zsem))
                return carry
            lax.fori_loop(0, tail_ref[0, 0, _M_ROUNDS + 1], fill_block, 0)


def _dispatch_call(plan, tail, selt, h2, cap):
    t_all, d = h2.shape
    n_tiles = t_all // TILE
    return pl.pallas_call(
        _dispatch_kernel,
        grid=(n_tiles,),
        in_specs=[pl.BlockSpec((1, 1, LANES), lambda i: (i, 0, 0), memory_space=pltpu.SMEM),
                  pl.BlockSpec((1, 1, LANES), lambda i: (0, 0, 0), memory_space=pltpu.SMEM),
                  pl.BlockSpec((1, 8, TILE), lambda i: (i, 0, 0)),
                  pl.BlockSpec((TILE, d), lambda i: (i, 0))],
        out_specs=pl.BlockSpec(memory_space=pl.ANY),
        out_shape=jax.ShapeDtypeStruct((cap, d // 2), U32),
        scratch_shapes=[pltpu.VMEM((2, SORT_W, d // 2), U32), pltpu.VMEM((MOE_BLK, d // 2), U32),
                        pltpu.SemaphoreType.DMA((2, N_EXP)), pltpu.SemaphoreType.DMA(()), pltpu.SMEM((1,), jnp.int32)],
        compiler_params=_cparams(("arbitrary",)),
        name="moe_dispatch",
    )(plan, tail, selt, h2)


GU_GRP = 256


def _regroup_kernel(w_ref, o_ref):
    rr = lax.broadcasted_iota(jnp.int32, (GU_GRP, GU_GRP), 0)
    cc = lax.broadcasted_iota(jnp.int32, (GU_GRP, GU_GRP), 1)
    half = GU_GRP // 2
    perm = (rr == jnp.where(cc < half, 2 * cc, 2 * (cc - half) + 1)).astype(MXU)
    for s in range(w_ref.shape[-1] // GU_GRP):
        sl = slice(s * GU_GRP, (s + 1) * GU_GRP)
        o_ref[:, sl] = _dot(w_ref[:, sl].astype(MXU), perm).astype(o_ref.dtype)


def _regroup_call(w_gu):
    n_exp, d, n2 = w_gu.shape
    tn = 512
    spec = pl.BlockSpec((None, d, tn), lambda e, j: (e, 0, j))
    return pl.pallas_call(
        _regroup_kernel,
        grid=(n_exp, n2 // tn),
        in_specs=[spec],
        out_specs=spec,
        out_shape=jax.ShapeDtypeStruct((n_exp, d, n2), MXU),
        compiler_params=_cparams(("parallel", "parallel")),
        name="regroup_gate_up",
    )(w_gu)


def _regroup_bias(b_gu):
    n_exp, n2 = b_gu.shape
    b = b_gu.astype(F32).reshape(n_exp, n2 // GU_GRP, GU_GRP // 2, 2)
    return jnp.concatenate([b[..., 0], b[..., 1]], axis=-1).reshape(n_exp, 1, n2)


def _expert_kernel(be_ref, nu_ref, xs_ref, wgu_ref, bgu_ref, wd_ref, bd_ref, y_ref):
    n = pl.program_id(0)
    half = GU_GRP // 2

    @pl.when(n < nu_ref[0])
    def _():
        xb = _unpack_pairs(xs_ref[...])
        gu = _dot(xb, wgu_ref[...]) + bgu_ref[...]
        acts = []
        for s in range(gu.shape[-1] // GU_GRP):
            g = jnp.minimum(gu[:, s * GU_GRP:s * GU_GRP + half], SWIGLU_LIMIT)
            u = jnp.clip(gu[:, s * GU_GRP + half:(s + 1) * GU_GRP], -SWIGLU_LIMIT, SWIGLU_LIMIT)
            acts.append(((u + 1.0) * g * jax.nn.sigmoid(SWIGLU_ALPHA * g)).astype(MXU))
        y = _dot(jnp.concatenate(acts, axis=1), wd_ref[...]) + bd_ref[...]
        y_ref[...] = _pack_pairs(y.astype(jnp.bfloat16).astype(F32))

    @pl.when(n >= nu_ref[0])
    def _():
        y_ref[...] = jnp.zeros(y_ref.shape, y_ref.dtype)


def _expert_call(blk_exp, n_used, xs, wgu, bgu, wd, bd):
    cap, dw = xs.shape
    d = 2 * dw
    dff2 = wgu.shape[-1]
    n_blk = cap // MOE_BLK
    emap = lambda n, be, nu: (be[n], 0, 0)
    rows = lambda n, be, nu: (n, 0)
    used_rows = lambda n, be, nu: (jnp.where(n < nu[0], n, 0), 0)
    return pl.pallas_call(
        _expert_kernel,
        grid_spec=pltpu.PrefetchScalarGridSpec(
            num_scalar_prefetch=2,
            grid=(n_blk,),
            in_specs=[pl.BlockSpec((MOE_BLK, dw), used_rows),
                      pl.BlockSpec((None, d, dff2), emap), pl.BlockSpec((None, 1, dff2), emap),
                      pl.BlockSpec((None, dff2 // 2, d), emap), pl.BlockSpec((None, 1, d), emap)],
            out_specs=pl.BlockSpec((MOE_BLK, dw), rows)),
        out_shape=jax.ShapeDtypeStruct((cap, dw), U32),
        compiler_params=_cparams(("arbitrary",)),
        name="expert_ffn",
    )(blk_exp, n_used, xs, wgu, bgu, wd, bd)


COMBINE_AHEAD = 2


def _combine_kernel(plan_ref, next_ref, ahead_ref, sel_ref, gate_ref, x1_ref, mod_ref, y_hbm, x2_o, stage, sem,
                    gf_ref=None):
    d = x1_ref.shape[-1]
    i = pl.program_id(0)
    n_buf = COMBINE_AHEAD + 1
    slot = i % n_buf
    sel = sel_ref[...]
    gate = gate_ref[...]
    lanes = lax.broadcasted_iota(jnp.int32, (TILE, TILE), 1)

    def run_copy(buf, e, src):
        return pltpu.make_async_copy(y_hbm.at[pl.ds(src, SORT_R), :],
                                     stage.at[buf, pl.ds(e * SORT_R, SORT_R), :], sem.at[buf])

    def weighted_sum(r, acc):
        cols = [_round_cols(sel[:, k:k + 1], sel[:, TOP_K + k:TOP_K + k + 1], r) for k in range(TOP_K)]
        for c in range(SORT_W // TILE):
            w = jnp.zeros((TILE, TILE), F32)
            for k in range(TOP_K):
                w = jnp.where(lanes + c * TILE == cols[k], gate[:, k:k + 1], w)
            acc = acc + _dot(w.astype(MXU), _unpack_pairs(stage[slot, c * TILE:(c + 1) * TILE, :]))
        return acc

    def fetch(buf, pref):
        for e in range(N_EXP):
            run_copy(buf, e, _run_start(pref, e, 0)).start(priority=e % 2)

    @pl.when(i == 0)
    def _():
        fetch(0, plan_ref)

    @pl.when((i == 0) & (pl.num_programs(0) > 1))
    def _():
        fetch(1, next_ref)

    @pl.when(i + COMBINE_AHEAD < pl.num_programs(0))
    def _():
        fetch((i + COMBINE_AHEAD) % n_buf, ahead_ref)

    for e in range(N_EXP):
        run_copy(slot, e, 0).wait()
    f = weighted_sum(0, jnp.zeros((TILE, d), F32))

    def extra_round(r, acc):
        _extra_round_copies(plan_ref, r, lambda e, src: run_copy(slot, e, src))
        return weighted_sum(r, acc)

    f = lax.fori_loop(1, plan_ref[0, 0, _M_ROUNDS], extra_round, f)
    x2 = x1_ref[...] + mod_ref[0][:, 5 * d:6 * d] * f
    if gf_ref is None:
        x2_o[...] = x2
    else:
        x2_o[...] = _rms(x2) * gf_ref[...]


def _combine_final_kernel(plan_ref, next_ref, ahead_ref, sel_ref, gate_ref, x1_ref, mod_ref, gf_ref, y_hbm,
                          out_o, stage, sem):
    _combine_kernel(plan_ref, next_ref, ahead_ref, sel_ref, gate_ref, x1_ref, mod_ref, y_hbm, out_o, stage, sem,
                    gf_ref=gf_ref)


def _combine_call(plan, sel, gate, x1, mod3, y, n_batch, g_final=None):
    t_all, d = x1.shape
    n_tiles = t_all // TILE
    ntb = n_tiles // n_batch
    tok, _, modr = _tile_maps(ntb, n_batch)
    if g_final is None:
        body, extra_in, extra_args = _combine_kernel, [], ()
        out_spec = pl.BlockSpec((TILE, d), tok)
        out_shape = jax.ShapeDtypeStruct((t_all, d), F32)
    else:
        body, extra_in, extra_args = _combine_final_kernel, [pl.BlockSpec((1, d), lambda i: (0, 0))], (g_final,)
        out_spec = pl.BlockSpec((None, TILE, d), lambda i: (i // ntb, jnp.maximum(i % ntb - 1, 0), 0))
        out_shape = jax.ShapeDtypeStruct((n_batch, (ntb - 1) * TILE, d), F32)
    return _combine_pallas(body, extra_in, extra_args, out_spec, out_shape, plan, sel, gate, x1, mod3, y, tok, modr)


def _combine_pallas(body, extra_in, extra_args, out_spec, out_shape, plan, sel, gate, x1, mod3, y, tok, modr):
    t_all, d = x1.shape
    n_tiles = t_all // TILE
    return pl.pallas_call(
        body,
        grid=(n_tiles,),
        in_specs=[pl.BlockSpec((1, 1, LANES), lambda i: (i, 0, 0), memory_space=pltpu.SMEM),
                  pl.BlockSpec((1, 1, LANES), lambda i: (jnp.minimum(i + 1, n_tiles - 1), 0, 0),
                               memory_space=pltpu.SMEM),
                  pl.BlockSpec((1, 1, LANES), lambda i: (jnp.minimum(i + COMBINE_AHEAD, n_tiles - 1), 0, 0),
                               memory_space=pltpu.SMEM),
                  pl.BlockSpec((TILE, LANES), tok), pl.BlockSpec((TILE, LANES), tok), pl.BlockSpec((TILE, d), tok),
                  pl.BlockSpec((1, 1, mod3.shape[-1]), modr)] + extra_in + [pl.BlockSpec(memory_space=pl.ANY)],
        out_specs=out_spec,
        out_shape=out_shape,
        scratch_shapes=[pltpu.VMEM((COMBINE_AHEAD + 1, SORT_W, d // 2), U32),
                        pltpu.SemaphoreType.DMA((COMBINE_AHEAD + 1,))],
        compiler_params=_cparams(("arbitrary",)),
        name="moe_combine",
    )(plan, plan, plan, sel, gate, x1, mod3, *extra_args, y)


def kernel(x, c, ctx, c_ctx, w_ada, b_ada, g_norm1, g_norm2, w_in, attn_sink, nat_rpb, mlstm_conv, mlstm_gate_bias, mla_g_q, mla_w_uq, mla_g_kv, mla_w_ukv, w_out, w_router, b_router, w_gu, b_gu, w_down, b_down, g_final):
    n_batch, s_len, d = x.shape
    ctx_len = ctx.shape[1]
    assert ctx_len == TILE and s_len % TILE == 0 and s_len // GRID_W >= NAT_KR
    depth = w_ada.shape[0]
    npb = ctx_len + s_len
    c_all = jnp.concatenate([c, c_ctx[None], jnp.zeros((8 - n_batch - 1, d), F32)], axis=0)
    mod_all = _mod_call(c_all, w_ada, b_ada)
    tabs = _rope_tables(s_len, ctx_len)
    x_all = jnp.concatenate([ctx, x], axis=1).reshape(n_batch * npb, d)
    n_le = depth * N_EXP
    ew = (_regroup_call(w_gu.reshape((n_le,) + w_gu.shape[2:])), _regroup_bias(b_gu.reshape(n_le, -1)),
          w_down.reshape((n_le,) + w_down.shape[2:]).astype(MXU), b_down.reshape(n_le, 1, d).astype(F32))
    for l in range(depth):
        pk = _pack_layer(w_in[l], mlstm_gate_bias[l], mla_w_uq[l], mla_w_ukv[l], mla_g_q[l], mla_g_kv[l],
                         w_out[l], mlstm_conv[l], nat_rpb[l])
        mod3 = mod_all[l].reshape(8, 1, 6 * d)
        (qa, ka, va, qb, kb, vb, qk, vc, oc, gi, gf, qd, kd, vd) = _proj_call(
            x_all, mod3, g_norm1[l].reshape(1, d), pk, tabs, n_batch)
        ya = _mixa_call(qa, ka, va, attn_sink[l], n_batch, s_len)
        yb = _mixb_call(qb, kb, vb, _nat_bias_table(pk["rpb"]), n_batch, s_len)
        hf, hb = _mixc_call(qk, vc, gi, gf, pk["conv"], n_batch)
        yd = _mixd_call(qd, kd, vd, n_batch)
        last = l == depth - 1
        x_all = _ffn_layer(x_all, mod3, ya, yb, hf, hb, oc, yd, pk["wo"], g_norm2[l], w_router[l], b_router[l],
                           ew, l * N_EXP, n_batch, g_final.reshape(1, d).astype(F32) if last else None)
    return x_all


def _ffn_layer(x_all, mod3, ya, yb, hf, hb, oc, yd, wo, g2, w_router, b_router, ew, exp_off, n_batch, g_final):
    t_all, d = x_all.shape
    wr = jnp.concatenate([w_router.astype(F32), jnp.zeros((d, LANES - N_EXP), F32)], axis=1)
    br = jnp.concatenate([b_router.astype(F32), jnp.full((LANES - N_EXP,), NEG, F32)]).reshape(1, LANES)
    x1, h2, sel, selt, gate, meta = _out_call(x_all, mod3, ya, yb, hf, hb, oc, yd, wo, g2.reshape(1, d),
                                              wr, br, n_batch)
    plan, tail, blk_exp, n_used, n_blk = _route_plan(meta, t_all)
    xs = _dispatch_call(plan, tail, selt, h2, n_blk * MOE_BLK)
    y = _expert_call(blk_exp + exp_off, n_used, xs, *ew)
    return _combine_call(plan, sel, gate, x1, mod3, y, n_batch, g_final)
```

```python
import functools

import jax
import jax.numpy as jnp
from jax import lax
from jax.experimental import pallas as pl
from jax.experimental.pallas import tpu as pltpu

F32 = jnp.float32
MXU = jnp.bfloat16

TILE = 256
PROJ_ROWS = 768
LANES = 128
HD = 64
GRID_W = 64
EPS = 1e-6
ROPE_BASE = 10000.0
NEG = -1e30
_LOG2E = 1.4426950408889634
D_VROWS = HD + 16

N_EXP = 32
TOP_K = 4
SWIGLU_LIMIT = 7.0
SWIGLU_ALPHA = 1.702
MOE_BLK = 256
RUN_ALIGN = 8

MLA_Q_LORA = 192
MLA_KV_LORA = 128
MLA_NOPE = 64
MLA_ROPE = 32
NAT_KR = 8
NAT_KC = 16

_IN_SIZES = (256, 128, 128, 256, 256, 256, 512, 256, 256, 16, 192, 128, 32)
_IN_OFF = tuple(sum(_IN_SIZES[:i]) for i in range(len(_IN_SIZES) + 1))

_C_QA, _C_QAR, _C_KA, _C_KAR, _C_VA = 0, 256, 512, 640, 768
_C_QB, _C_KB, _C_VB = 896, 1152, 1408
_C_QK, _C_VC, _C_OC, _C_GI, _C_GF = 1664, 2176, 2432, 2688, 2816
_C_CQ, _C_CKV, _C_KR, _C_KRR = 2944, 3200, 3328, 3456
_NC = 3584

_VMEM_LIMIT = 56 * 1024 * 1024


def _cparams(sem):
    return pltpu.CompilerParams(dimension_semantics=sem, vmem_limit_bytes=_VMEM_LIMIT)


def _dot(a, b):
    return jnp.dot(a, b, preferred_element_type=F32)


def _dot_nt(a, b):
    return lax.dot_general(a, b, (((1,), (1,)), ((), ())), preferred_element_type=F32)


def _split3(x):
    hi = x.astype(MXU)
    r1 = x - hi.astype(F32)
    mid = r1.astype(MXU)
    lo = (r1 - mid.astype(F32)).astype(MXU)
    return hi, mid, lo


def _head_cols(w, order, swap):
    d = w.shape[0]
    nh = w.shape[1] // HD
    w = jnp.stack([w.reshape(d, nh, HD)[:, i, :] for i in order], axis=1)
    if swap:
        w = jnp.concatenate([w[..., HD // 2:], w[..., :HD // 2]], axis=-1)
    return w.reshape(d, len(order) * HD)


def _pack_layer(w_in, gate_bias, w_uq, w_ukv, g_q, g_kv, w_out, conv_w, rpb):
    d = w_in.shape[0]
    o = _IN_OFF
    col = lambda i: w_in[:, o[i]:o[i + 1]]
    z = lambda n: jnp.zeros((d, n), F32)
    qa, ka, va, qb, kb, vb, qk, vc, oc, g, cq, ckv, kr = [col(i) for i in range(13)]
    a_ord = (0, 2, 1, 3)
    sc = HD ** -0.5
    gI = jnp.concatenate([g[:, 0:4], g[:, 8:12], z(120)], axis=1)
    gF = jnp.concatenate([g[:, 4:8], g[:, 12:16], z(120)], axis=1)
    hr = MLA_ROPE // 2
    kr_g = jnp.concatenate([z(64), kr, z(32)], axis=1)
    krr_g = jnp.concatenate([z(64), kr[:, hr:], kr[:, :hr], z(32)], axis=1)
    w1 = jnp.concatenate([
        _head_cols(qa, a_ord, False) * sc, _head_cols(qa, a_ord, True) * sc,
        ka, _head_cols(ka, (0, 1), True), va,
        qb * sc, kb, vb,
        qk, vc, oc, gI, gF,
        cq, z(64), ckv, kr_g, krr_g], axis=1)
    assert w1.shape[1] == _NC
    gb = gate_bias.astype(F32)
    gbias = jnp.stack([jnp.concatenate([gb[0], gb[2], jnp.zeros((120,), F32)]),
                       jnp.concatenate([gb[1], gb[3], jnp.zeros((120,), F32)])])
    dq = MLA_NOPE + MLA_ROPE
    sd = dq ** -0.5
    qh, qrh = [], []
    for h in range(4):
        nope = w_uq[:, h * dq:h * dq + MLA_NOPE]
        rope = w_uq[:, h * dq + MLA_NOPE:(h + 1) * dq]
        zq = lambda n: jnp.zeros((MLA_Q_LORA, n), F32)
        qh.append(jnp.concatenate([nope, rope, zq(32)], axis=1))
        qrh.append(jnp.concatenate([zq(64), rope[:, hr:], rope[:, :hr], zq(32)], axis=1))
    wuq = jnp.concatenate(qh + qrh, axis=1) * (sd * _LOG2E)
    wuq = jnp.concatenate([wuq, jnp.zeros((64, 1024), F32)], axis=0)
    kh = []
    for h in range(4):
        kn = w_ukv[:, h * 128:h * 128 + MLA_NOPE]
        kh.append(jnp.concatenate([kn, jnp.zeros((MLA_KV_LORA, 64), F32)], axis=1))
    vh = [w_ukv[:, h * 128 + MLA_NOPE:(h + 1) * 128] for h in range(4)]
    wukv = jnp.concatenate(kh + vh, axis=1)
    gq = jnp.concatenate([g_q.astype(F32), jnp.zeros((64,), F32)]).reshape(1, 256)
    gkv = g_kv.astype(F32).reshape(1, 128)
    wo = jnp.concatenate([w_out[i * HD:(i + 1) * HD] for i in a_ord] + [w_out[256:]], axis=0)
    return dict(w1=w1.astype(MXU), gbias=gbias, wuq=wuq.astype(MXU), wukv=wukv.astype(MXU),
                gq=gq, gkv=gkv, wo=wo.astype(MXU), conv=conv_w.astype(F32), rpb=rpb.astype(F32))


def _rope_tables(s_len, ctx_len):
    t = jnp.arange(s_len, dtype=jnp.int32)
    rows = (t // GRID_W).astype(F32)[:, None]
    cols = (t % GRID_W).astype(F32)[:, None]

    def ang(rot_dim):
        d_ax = rot_dim // 2
        inv = ROPE_BASE ** (-jnp.arange(0, d_ax, 2, dtype=F32) / d_ax)
        return jnp.concatenate([rows * inv, cols * inv], axis=-1)

    aa = ang(HD)
    ca, sa = jnp.cos(aa), jnp.sin(aa)
    cos_a = jnp.tile(jnp.concatenate([ca, ca], axis=-1), (1, 2))
    sin_a = jnp.tile(jnp.concatenate([-sa, sa], axis=-1), (1, 2))
    ad = ang(MLA_ROPE)
    cd, sd = jnp.cos(ad), jnp.sin(ad)
    one = lambda n: jnp.ones((s_len, n), F32)
    zero = lambda n: jnp.zeros((s_len, n), F32)
    cos_d = jnp.concatenate([one(64), cd, cd, one(32)], axis=-1)
    sin_d = jnp.concatenate([zero(64), -sd, sd, zero(32)], axis=-1)
    pad = lambda a, v: jnp.concatenate([jnp.full((ctx_len, LANES), v, F32), a], axis=0)
    return pad(cos_a, 1.0), pad(sin_a, 0.0), pad(cos_d, 1.0), pad(sin_d, 0.0)


def _mod_kernel(c_ref, w_ref, b_ref, o_ref):
    c = c_ref[...]
    s = c * jax.nn.sigmoid(c)
    w = w_ref[0]
    acc = jnp.zeros(o_ref.shape[1:], F32)
    sp = _split3(s)
    wp = _split3(w)
    for i, j in ((0, 0), (0, 1), (1, 0), (1, 1), (0, 2), (2, 0)):
        acc = acc + _dot(sp[i], wp[j])
    o_ref[0] = acc + b_ref[0]


def _mod_call(c_all, w_ada, b_ada):
    depth, d, n = w_ada.shape
    tn = 768
    return pl.pallas_call(
        _mod_kernel,
        grid=(depth, n // tn),
        in_specs=[pl.BlockSpec((8, d), lambda l, j: (0, 0)),
                  pl.BlockSpec((1, d, tn), lambda l, j: (l, 0, j)),
                  pl.BlockSpec((1, 1, tn), lambda l, j: (l, 0, j))],
        out_specs=pl.BlockSpec((1, 8, tn), lambda l, j: (l, 0, j)),
        out_shape=jax.ShapeDtypeStruct((depth, 8, n), F32),
        compiler_params=_cparams(("parallel", "parallel")),
        name="adaln_mod",
    )(c_all, w_ada, b_ada.reshape(depth, 1, n))


def _rms(x, n=None):
    n = x.shape[-1] if n is None else n
    return x * lax.rsqrt(jnp.sum(x * x, axis=-1, keepdims=True) * (1.0 / n) + EPS)


def _proj_kernel(x_ref, mod_ref, modc_ref, g1_ref, w1_ref, gb_ref, wuq_ref, wukv_ref, gq_ref, gkv_ref,
                 ca_ref, sa_ref, cd_ref, sd_ref,
                 qa_o, ka_o, va_o, qb_o, kb_o, vb_o, qk_o, vc_o, oc_o, gi_o, gf_o, qd_o, kd_o, vd_o, *, tiles_b):
    d = x_ref.shape[-1]
    x = x_ref[...]
    row = lax.broadcasted_iota(jnp.int32, (x.shape[0], 1), 0)
    is_ctx = (pl.program_id(0) % tiles_b == 0) & (row < TILE)
    mod = jnp.where(is_ctx, modc_ref[0][:, 0:2 * d], mod_ref[0][:, 0:2 * d])
    h = _rms(x) * g1_ref[...]
    h = (h * (1.0 + mod[:, d:2 * d]) + mod[:, 0:d]).astype(MXU)
    proj = lambda a, n: _dot(h, w1_ref[:, a:a + n])
    ca, sa, cd, sd = ca_ref[...], sa_ref[...], cd_ref[...], sd_ref[...]

    q, qr = proj(_C_QA, 256), proj(_C_QAR, 256)
    for g in range(2):
        sl = slice(g * LANES, (g + 1) * LANES)
        qa_o[:, sl] = (q[:, sl] * ca + qr[:, sl] * sa).astype(qa_o.dtype)
    ka_o[...] = (proj(_C_KA, 128) * ca + proj(_C_KAR, 128) * sa).astype(ka_o.dtype)
    va_o[...] = proj(_C_VA, 128).astype(va_o.dtype)
    qb_o[...] = proj(_C_QB, 256).astype(qb_o.dtype)
    kb_o[...] = proj(_C_KB, 256).astype(kb_o.dtype)
    vb_o[...] = proj(_C_VB, 256).astype(vb_o.dtype)
    qk_o[...] = proj(_C_QK, 512).astype(qk_o.dtype)
    vc_o[...] = proj(_C_VC, 256).astype(vc_o.dtype)
    oc_o[...] = proj(_C_OC, 256).astype(oc_o.dtype)
    gb = gb_ref[...]
    gi_o[...] = proj(_C_GI, 128) + gb[0:1, :]
    gf_o[...] = proj(_C_GF, 128) + gb[1:2, :]
    cq = proj(_C_CQ, 256)
    cqn = (_rms(cq, MLA_Q_LORA) * gq_ref[...]).astype(MXU)
    q2 = _dot(cqn, wuq_ref[...])
    ckv = proj(_C_CKV, 128)
    ckvn = (_rms(ckv) * gkv_ref[...]).astype(MXU)
    kv2 = _dot(ckvn, wukv_ref[...])
    krope = proj(_C_KR, 128) * cd + proj(_C_KRR, 128) * sd
    for hh in range(4):
        sl = slice(hh * LANES, (hh + 1) * LANES)
        sr = slice(512 + hh * LANES, 512 + (hh + 1) * LANES)
        qd_o[:, sl] = (q2[:, sl] * cd + q2[:, sr] * sd).astype(qd_o.dtype)
        kd_o[:, sl] = (kv2[:, sl] + krope).astype(kd_o.dtype)
    vt = kv2[:, 512:768].T
    ones = jnp.ones((D_VROWS - HD, vt.shape[1]), F32)
    for hh in range(4):
        vd_o[hh * D_VROWS:(hh + 1) * D_VROWS, :] = jnp.concatenate(
            [vt[hh * HD:(hh + 1) * HD, :], ones], axis=0).astype(vd_o.dtype)


def _tile_maps(n_tiles_b, n_batch):
    tok = lambda i: (i, 0)
    pos = lambda i: (i % n_tiles_b, 0)
    modr = lambda i: (jnp.where(i % n_tiles_b == 0, n_batch, i // n_tiles_b), 0, 0)
    return tok, pos, modr


def _proj_call(x_all, mod3, g1, pk, tabs, n_batch):
    t_all, d = x_all.shape
    npb = t_all // n_batch
    rt = PROJ_ROWS if npb % PROJ_ROWS == 0 else TILE
    n_tiles = t_all // rt
    ntb = n_tiles // n_batch
    tok = lambda i: (i, 0)
    pos = lambda i: (i % ntb, 0)
    const = lambda i: (0, 0)
    full = lambda a: pl.BlockSpec(a.shape, const)
    widths = (256, 128, 128, 256, 256, 256, 512, 256, 256, 128, 128, 512, 512)
    dts = (MXU,) * 9 + (F32, F32) + (MXU,) * 2
    vt_spec = pl.BlockSpec((None, 4 * D_VROWS, rt), lambda i: (i // ntb, 0, i % ntb))
    vt_shape = jax.ShapeDtypeStruct((n_batch, 4 * D_VROWS, npb), MXU)
    tab = pl.BlockSpec((rt, LANES), pos)
    modw = mod3.shape[-1]
    return pl.pallas_call(
        functools.partial(_proj_kernel, tiles_b=ntb),
        grid=(n_tiles,),
        in_specs=[pl.BlockSpec((rt, d), tok), pl.BlockSpec((1, 1, modw), lambda i: (i // ntb, 0, 0)),
                  pl.BlockSpec((1, 1, modw), lambda i: (n_batch, 0, 0)), full(g1),
                  full(pk["w1"]), full(pk["gbias"]), full(pk["wuq"]), full(pk["wukv"]),
                  full(pk["gq"]), full(pk["gkv"]), tab, tab, tab, tab],
        out_specs=[pl.BlockSpec((rt, w), tok) for w in widths] + [vt_spec],
        out_shape=[jax.ShapeDtypeStruct((t_all, w), dt) for w, dt in zip(widths, dts)] + [vt_shape],
        compiler_params=_cparams(("parallel",)),
        name="in_proj",
    )(x_all, mod3, mod3, g1, pk["w1"], pk["gbias"], pk["wuq"], pk["wukv"], pk["gq"], pk["gkv"], *tabs)


_A_ORD = (0, 2, 1, 3)
A_BLK = 128
A_WIN = 128


def _mixa_kernel(sink_ref, q_ref, kp_ref, kc_ref, kn_ref, kx_ref, vp_ref, vc_ref, vn_ref, vx_ref, o_ref,
                 *, s_len, ctx_blocks):
    n = pl.program_id(1)
    is_lat = n >= ctx_blocks
    nb = 3 * A_BLK
    lane = lax.broadcasted_iota(jnp.int32, (A_BLK, LANES), 1)
    lo = lane < HD
    qi = lax.broadcasted_iota(jnp.int32, (2 * A_BLK, nb), 0) & (A_BLK - 1)
    kj = lax.broadcasted_iota(jnp.int32, (2 * A_BLK, nb), 1)
    qpos = (n - ctx_blocks) * A_BLK + qi
    kpos = (n - ctx_blocks - 1) * A_BLK + kj
    ok = is_lat & (kpos >= 0) & (kpos < s_len) & (jnp.abs(qpos - kpos) <= A_WIN)
    row = lax.broadcasted_iota(jnp.int32, (2 * A_BLK, 1), 0)
    k_all = jnp.concatenate([kp_ref[...], kc_ref[...], kn_ref[...], kx_ref[...]], axis=0)
    v_all = jnp.concatenate([vp_ref[...], vc_ref[...], vn_ref[...], vx_ref[...]], axis=0)
    q = q_ref[...]
    zero = jnp.zeros((A_BLK, LANES), q.dtype)
    scores = []
    for g in range(2):
        qg = q[:, g * LANES:(g + 1) * LANES]
        q2 = jnp.concatenate([jnp.where(lo, qg, zero), jnp.where(lo, zero, qg)], axis=0)
        scores.append(_dot_nt(q2, k_all))
    for g in range(2):
        s = scores[g]
        s = jnp.concatenate([jnp.where(ok, s[:, :nb], NEG), s[:, nb:]], axis=1)
        sink = jnp.where(row < A_BLK, sink_ref[_A_ORD[2 * g]], sink_ref[_A_ORD[2 * g + 1]])
        m = jnp.maximum(jnp.max(s, axis=-1, keepdims=True), sink)
        p = jnp.exp(s - m)
        den = jnp.sum(p, axis=-1, keepdims=True) + jnp.exp(sink - m)
        pv = _dot(p.astype(MXU), v_all) / den
        o_ref[:, g * LANES:(g + 1) * LANES] = jnp.where(lo, pv[:A_BLK], pv[A_BLK:]).astype(o_ref.dtype)


def _mixa_call(qa, ka, va, sink, n_batch, s_len):
    t_all = qa.shape[0]
    nq = t_all // n_batch // A_BLK
    cb = TILE // A_BLK
    qmap = lambda b, n: (b * nq + n, 0)
    pmap = lambda b, n: (b * nq + jnp.clip(n - 1, cb, nq - 1), 0)
    nmap = lambda b, n: (b * nq + jnp.clip(n + 1, cb, nq - 1), 0)
    xmap = lambda b, n: (b * (nq // cb), 0)
    kv = lambda m: pl.BlockSpec((A_BLK, LANES), m)
    kvx = pl.BlockSpec((TILE, LANES), xmap)
    return pl.pallas_call(
        functools.partial(_mixa_kernel, s_len=s_len, ctx_blocks=cb),
        grid=(n_batch, nq),
        in_specs=[pl.BlockSpec(memory_space=pltpu.SMEM), pl.BlockSpec((A_BLK, 256), qmap),
                  kv(pmap), kv(qmap), kv(nmap), kvx, kv(pmap), kv(qmap), kv(nmap), kvx],
        out_specs=pl.BlockSpec((A_BLK, 256), qmap),
        out_shape=jax.ShapeDtypeStruct((t_all, 256), MXU),
        compiler_params=_cparams(("parallel", "parallel")),
        name="mixer_a",
    )(sink.astype(F32), qa, ka, ka, ka, ka, va, va, va, va)


def _nat_bias_table(rpb):
    qc = jnp.arange(GRID_W)[:, None]
    kc = jnp.arange(GRID_W)[None, :]
    cs = jnp.clip(qc - NAT_KC // 2, 0, GRID_W - NAT_KC)
    ok = (kc >= cs) & (kc < cs + NAT_KC)
    dc = jnp.clip(kc - qc + NAT_KC - 1, 0, 2 * NAT_KC - 2)
    t = jnp.where(ok[None, None], rpb[:, :, dc], NEG)
    pair = lambda h: jnp.concatenate([t[h, :-1], t[h, 1:]], axis=-1)
    return jnp.stack([jnp.concatenate([pair(2 * g), pair(2 * g + 1)], axis=1) for g in range(2)])


def _mixb_kernel(q_ref, k_ref, v_ref, tb_ref, o_ref, *, rows):
    j = pl.program_id(1)
    is_ctx = j == 0
    nloc = NAT_KR * GRID_W
    lane = lax.broadcasted_iota(jnp.int32, (GRID_W, LANES), 1)
    lo = lane < HD
    zero = jnp.zeros((GRID_W, LANES), q_ref.dtype)
    for i in range(TILE // GRID_W):
        r = jnp.maximum((j - 1) * (TILE // GRID_W) + i, 0)
        r0 = jnp.clip(r - NAT_KR // 2, 0, rows - NAT_KR)
        kstart = pl.multiple_of(TILE + r0 * GRID_W, GRID_W)
        dr0 = r0 - r + NAT_KR - 1
        for g in range(2):
            sl = slice(g * LANES, (g + 1) * LANES)
            qg = q_ref[i * GRID_W:(i + 1) * GRID_W, sl]
            q2 = jnp.concatenate([jnp.where(lo, qg, zero), jnp.where(lo, zero, qg)], axis=0)
            s_loc = _dot_nt(q2, k_ref[pl.ds(kstart, nloc), sl])
            s_ctx = _dot_nt(q2, k_ref[0:TILE, sl])
            bias = jnp.concatenate([tb_ref[g, dr0 + 2 * t] for t in range(NAT_KR // 2)], axis=1)
            s_loc = jnp.where(is_ctx, NEG, s_loc + bias)
            m = jnp.maximum(jnp.max(s_loc, axis=-1, keepdims=True), jnp.max(s_ctx, axis=-1, keepdims=True))
            p_loc = jnp.exp(s_loc - m)
            p_ctx = jnp.exp(s_ctx - m)
            den = jnp.sum(p_loc, axis=-1, keepdims=True) + jnp.sum(p_ctx, axis=-1, keepdims=True)
            pv = _dot(p_loc.astype(MXU), v_ref[pl.ds(kstart, nloc), sl]) + _dot(p_ctx.astype(MXU), v_ref[0:TILE, sl])
            pv = pv / den
            o_ref[i * GRID_W:(i + 1) * GRID_W, sl] = jnp.where(lo, pv[:GRID_W], pv[GRID_W:]).astype(o_ref.dtype)


def _mixb_call(qb, kb, vb, tb, n_batch, s_len):
    t_all = qb.shape[0]
    npb = t_all // n_batch
    ntb = npb // TILE
    qmap = lambda b, j: (b * ntb + j, 0)
    kvmap = lambda b, j: (b, 0, 0)
    kvspec = pl.BlockSpec((None, npb, 256), kvmap)
    return pl.pallas_call(
        functools.partial(_mixb_kernel, rows=s_len // GRID_W),
        grid=(n_batch, ntb),
        in_specs=[pl.BlockSpec((TILE, 256), qmap), kvspec, kvspec,
                  pl.BlockSpec(tb.shape, lambda b, j: (0, 0, 0, 0))],
        out_specs=pl.BlockSpec((TILE, 256), qmap),
        out_shape=jax.ShapeDtypeStruct((t_all, 256), MXU),
        compiler_params=_cparams(("parallel", "arbitrary")),
        name="mixer_b",
    )(qb, kb.reshape(n_batch, npb, 256), vb.reshape(n_batch, npb, 256), tb)


D_TK = 256
D_UNROLL = 16


def _mixd_kernel(q_ref, k_ref, vt_ref, o_ref, acc_sc):
    j = pl.program_id(1)
    n_lat = (pl.num_programs(1) - 1) * TILE
    n_rest = jnp.where(j == 0, 0, n_lat // (D_TK * D_UNROLL))
    acc_sc[...] = jnp.zeros(acc_sc.shape, F32)
    qs = [q_ref[:, h * LANES:(h + 1) * LANES] for h in range(4)]

    def steps(starts, tk, ms):
        ms = list(ms)
        sts = [[_dot_nt(k_ref[pl.ds(ks, tk), h * LANES:(h + 1) * LANES], qs[h]) for h in range(4)]
               for ks in starts]
        for ks, st4 in zip(starts, sts):
            for h in range(4):
                st = st4[h].astype(MXU)
                m_new = jnp.maximum(ms[h], jnp.max(st, axis=0, keepdims=True).astype(F32))
                alpha = jnp.exp2(ms[h] - m_new)
                pt = jnp.exp2(st - m_new.astype(MXU))
                ms[h] = m_new
                pv = _dot(vt_ref[h * D_VROWS:(h + 1) * D_VROWS, pl.ds(ks, tk)], pt)
                acc_sc[h] = acc_sc[h] * alpha + pv
        return tuple(ms)

    init = tuple(jnp.full((1, TILE), NEG, F32) for _ in range(4))
    carry = steps([0], TILE, init)
    body = lambda i, c: steps([pl.multiple_of(TILE + (D_UNROLL * i + u) * D_TK, TILE) for u in range(D_UNROLL)],
                              D_TK, c)
    lax.fori_loop(0, n_rest, body, carry)
    for p in range(2):
        outs = [acc_sc[h, 0:HD, :] / acc_sc[h, HD:HD + 1, :] for h in (2 * p, 2 * p + 1)]
        o_ref[:, p * LANES:(p + 1) * LANES] = jnp.concatenate(outs, axis=0).T.astype(o_ref.dtype)


def _mixd_call(qd, kd, vdt, n_batch):
    t_all = qd.shape[0]
    npb = t_all // n_batch
    ntb = npb // TILE
    assert ((ntb - 1) * TILE) % (D_TK * D_UNROLL) == 0
    qmap = lambda b, j: (b * ntb + j, 0)
    kvmap = lambda b, j: (b, 0, 0)
    return pl.pallas_call(
        _mixd_kernel,
        grid=(n_batch, ntb),
        in_specs=[pl.BlockSpec((TILE, 512), qmap), pl.BlockSpec((None, npb, 512), kvmap),
                  pl.BlockSpec((None, 4 * D_VROWS, npb), kvmap)],
        out_specs=pl.BlockSpec((TILE, 256), qmap),
        out_shape=jax.ShapeDtypeStruct((t_all, 256), MXU),
        scratch_shapes=[pltpu.VMEM((4, D_VROWS, TILE), F32)],
        compiler_params=_cparams(("parallel", "arbitrary")),
        name="mixer_d",
    )(qd, kd.reshape(n_batch, npb, 512), vdt)


C_L = 128
HALO = 16


def _log_sigmoid(x):
    return jnp.minimum(x, 0.0) - jnp.log1p(jnp.exp(-jnp.abs(x)))


def _mlstm_prep(d, j, cb, nch, qk_ref, hp_ref, hn_ref, v_ref, gi_ref, gf_ref, conv_ref, c_sc, n_sc):
    row = lax.broadcasted_iota(jnp.int32, (C_L, 1), 0)
    rr = lax.broadcasted_iota(jnp.int32, (C_L, C_L), 0)
    cc = lax.broadcasted_iota(jnp.int32, (C_L, C_L), 1)
    lo = cc < HD
    causal = (cc <= rr) if d == 0 else (cc >= rr)

    prev_ok = (j != 0) & (j != cb)
    next_ok = (j != cb - 1) & (j != nch - 1)
    x = qk_ref[...].astype(F32)
    prow = jnp.where(prev_ok, hp_ref[HALO - 1:HALO, :].astype(F32), 0.0)
    nrow = jnp.where(next_ok, hn_ref[0:1, :].astype(F32), 0.0)
    xm1 = jnp.where(row == 0, prow, pltpu.roll(x, 1, 0))
    xp1 = jnp.where(row == C_L - 1, nrow, pltpu.roll(x, C_L - 1, 0))
    w = conv_ref[...]
    u = xm1 * w[0:1, :] + x * w[1:2, :] + xp1 * w[2:3, :]
    a = u * jax.nn.sigmoid(u)
    q_all = a[:, 0:256].astype(MXU)
    k_all = a[:, 256:512] * (HD ** -0.5)

    f = _log_sigmoid(gf_ref[...])
    bc = sum(_dot(causal.astype(MXU), part) for part in _split3(f))
    g = gi_ref[...] - bc
    st = dict(d=d, row=row, lo=lo, causal=causal, blockdiag=(rr < HD) == (cc < HD), bc=bc, g=g, gt=g.T,
              last=C_L - 1 if d == 0 else 0, units=[], pairs=[])
    for p in range(2):
        sl = slice(p * LANES, (p + 1) * LANES)
        qp = q_all[:, sl]
        kp32 = k_all[:, sl]
        kp = kp32.astype(MXU)
        cmat, nmat = c_sc[d, p], n_sc[d, p]
        c16, n16 = cmat.astype(MXU), nmat.astype(MXU)
        zero = jnp.zeros_like(qp)
        st["pairs"].append(dict(sl=sl, kp32=kp32, vp=v_ref[:, sl], cmat=cmat, nmat=nmat))
        for hh in range(2):
            qh = jnp.where(lo, qp, zero) if hh == 0 else jnp.where(lo, zero, qp)
            st["units"].append(dict(ln=d * 4 + 2 * p + hh, s=_dot_nt(qh, kp), qc=_dot(qh, c16), qn=_dot(qh, n16)))
    return st


def _mlstm_gates(st, m_sc):
    causal, bc, g, gt, last = st["causal"], st["bc"], st["g"], st["gt"], st["last"]
    for un in st["units"]:
        ln = un["ln"]
        bcol = bc[:, ln:ln + 1]
        logd = jnp.where(causal, bcol + gt[ln:ln + 1, :], NEG)
        mloc = jnp.max(logd, axis=-1, keepdims=True)
        m_prev = m_sc[ln:ln + 1, 0:1]
        m_t = jnp.maximum(bcol + m_prev, mloc)
        sw = un["s"] * jnp.exp(logd - m_t)
        a_end = mloc[last:last + 1, :]
        b_end = bcol[last:last + 1, :]
        m_new = jnp.maximum(b_end + m_prev, a_end)
        un.update(sw=sw, m_t=m_t, inter=jnp.exp(bcol + m_prev - m_t), m_new=m_new,
                  decay=jnp.exp(b_end + m_prev - m_new), scl=jnp.exp(a_end - m_new),
                  wcol=jnp.exp(g[:, ln:ln + 1] + (b_end - a_end)))


def _mlstm_finish(st, o_ref, c_sc, n_sc, m_sc):
    d, lo, row = st["d"], st["lo"], st["row"]
    for p, pr in enumerate(st["pairs"]):
        us = st["units"][2 * p:2 * p + 2]
        sws = [un["sw"].astype(MXU) for un in us]
        nums = [_dot(sw, pr["vp"]) for sw in sws]
        ones = jnp.ones((C_L, LANES), MXU)
        dens = [_dot(sw, ones) for sw in sws]
        kwt = (pr["kp32"] * jnp.where(lo, us[0]["wcol"], us[1]["wcol"])).T.astype(MXU)
        kv = jnp.where(st["blockdiag"], _dot(kwt, pr["vp"]), 0.0)
        ksum = _dot(kwt, ones)
        houts = []
        for un, num, dsum in zip(us, nums, dens):
            den = dsum + un["inter"] * un["qn"]
            houts.append((num + un["inter"] * un["qc"]) / jnp.maximum(jnp.abs(den), jnp.exp(-un["m_t"])))
            m_sc[un["ln"]:un["ln"] + 1, :] = jnp.broadcast_to(un["m_new"], (1, LANES))
        o_ref[:, pr["sl"]] = jnp.where(lo, houts[0], houts[1])
        dec = jnp.where(row < HD, us[0]["decay"], us[1]["decay"])
        scl = jnp.where(row < HD, us[0]["scl"], us[1]["scl"])
        c_sc[d, p] = dec * pr["cmat"] + scl * kv
        n_sc[d, p] = dec * pr["nmat"] + scl * ksum


def _mixc_kernel(qkf, hpf, hnf, vf, gif, gff, qkb, hpb, hnb, vb, gib, gfb, conv_ref, of_ref, ob_ref,
                 c_sc, n_sc, m_sc, *, cb, nch):
    i = pl.program_id(1)

    @pl.when(i == 0)
    def _():
        c_sc[...] = jnp.zeros(c_sc.shape, F32)
        n_sc[...] = jnp.zeros(n_sc.shape, F32)
        m_sc[...] = jnp.zeros(m_sc.shape, F32)

    jb = jnp.where(i < cb, cb - 1 - i, nch + cb - 1 - i)
    sf = _mlstm_prep(0, i, cb, nch, qkf, hpf, hnf, vf, gif, gff, conv_ref, c_sc, n_sc)
    sb = _mlstm_prep(1, jb, cb, nch, qkb, hpb, hnb, vb, gib, gfb, conv_ref, c_sc, n_sc)
    _mlstm_gates(sf, m_sc)
    _mlstm_gates(sb, m_sc)
    _mlstm_finish(sf, of_ref, c_sc, n_sc, m_sc)
    _mlstm_finish(sb, ob_ref, c_sc, n_sc, m_sc)


def _mixc_call(qk, vc, gi, gf, conv_w, n_batch):
    t_all = qk.shape[0]
    nch = t_all // n_batch // C_L
    cb = TILE // C_L
    hb = C_L // HALO
    n_halo = t_all // HALO
    jf = lambda i: i
    jb = lambda i: jnp.where(i < cb, cb - 1 - i, nch + cb - 1 - i)

    def specs(jmap):
        cur = lambda b, i: (b * nch + jmap(i), 0)
        prv = lambda b, i: (jnp.maximum((b * nch + jmap(i)) * hb - 1, 0), 0)
        nxt = lambda b, i: (jnp.minimum((b * nch + jmap(i) + 1) * hb, n_halo - 1), 0)
        return cur, [pl.BlockSpec((C_L, 512), cur), pl.BlockSpec((HALO, 512), prv), pl.BlockSpec((HALO, 512), nxt),
                     pl.BlockSpec((C_L, 256), cur), pl.BlockSpec((C_L, LANES), cur), pl.BlockSpec((C_L, LANES), cur)]

    cur_f, in_f = specs(jf)
    cur_b, in_b = specs(jb)
    args = (qk, qk, qk, vc, gi, gf)
    return pl.pallas_call(
        functools.partial(_mixc_kernel, cb=cb, nch=nch),
        grid=(n_batch, nch),
        in_specs=in_f + in_b + [pl.BlockSpec(conv_w.shape, lambda b, i: (0, 0))],
        out_specs=[pl.BlockSpec((C_L, 256), cur_f), pl.BlockSpec((C_L, 256), cur_b)],
        out_shape=[jax.ShapeDtypeStruct((t_all, 256), F32)] * 2,
        scratch_shapes=[pltpu.VMEM((2, 2, LANES, LANES), F32), pltpu.VMEM((2, 2, LANES, LANES), F32),
                        pltpu.VMEM((8, LANES), F32)],
        compiler_params=_cparams(("parallel", "arbitrary")),
        name="mixer_c",
    )(*args, *args, conv_w)


def _out_kernel(x_ref, mod_ref, ya_ref, yb_ref, hf_ref, hb_ref, oc_ref, yd_ref, wo_ref, g2_ref, wr_ref, br_ref,
                x1_o, h2_o, sel_o, selt_o, gate_o, meta_o, carry_sc):
    d = x_ref.shape[-1]
    i = pl.program_id(0)

    @pl.when(i == 0)
    def _():
        carry_sc[...] = jnp.zeros(carry_sc.shape, F32)

    mod = mod_ref[0]
    ym = (jax.nn.sigmoid(oc_ref[...].astype(F32)) * (hf_ref[...] + hb_ref[...])).astype(MXU)
    acc = (_dot(ya_ref[...], wo_ref[0:256, :]) + _dot(yb_ref[...], wo_ref[256:512, :])
           + _dot(ym, wo_ref[512:768, :]) + _dot(yd_ref[...], wo_ref[768:1024, :]))
    x1 = x_ref[...] + mod[:, 2 * d:3 * d] * acc
    x1_o[...] = x1
    h2 = _rms(x1) * g2_ref[...]
    h2 = h2 * (1.0 + mod[:, 4 * d:5 * d]) + mod[:, 3 * d:4 * d]
    h2_o[...] = h2.astype(h2_o.dtype)

    hp = _split3(h2)
    wp = _split3(wr_ref[...])
    logits = br_ref[...]
    for a, b in ((0, 0), (0, 1), (1, 0), (1, 1), (0, 2), (2, 0)):
        logits = logits + _dot(hp[a], wp[b])
    lane = lax.broadcasted_iota(jnp.int32, (TILE, LANES), 1)
    vals, idxs = [], []
    cur = logits
    for _ in range(TOP_K):
        mk = jnp.max(cur, axis=-1, keepdims=True)
        ik = jnp.min(jnp.where(cur == mk, lane, LANES), axis=-1, keepdims=True)
        cur = jnp.where(lane == ik, NEG, cur)
        vals.append(mk)
        idxs.append(ik)
    es = [jnp.exp(v - vals[0]) for v in vals]
    esum = es[0] + es[1] + es[2] + es[3]
    hot = [(lane == ik) for ik in idxs]
    multi = sum(h.astype(F32) for h in hot)
    rr = lax.broadcasted_iota(jnp.int32, (TILE, TILE), 0)
    cc = lax.broadcasted_iota(jnp.int32, (TILE, TILE), 1)
    before = _dot((cc < rr).astype(MXU), multi.astype(MXU))
    tile_cnt = jnp.sum(multi, axis=0, keepdims=True)
    sel_t = jnp.zeros((TILE, LANES), F32)
    gate_t = jnp.zeros((TILE, LANES), F32)
    for k in range(TOP_K):
        lr = jnp.sum(jnp.where(hot[k], before, 0.0), axis=-1, keepdims=True)
        sel_t = jnp.where(lane == k, idxs[k].astype(F32), sel_t)
        sel_t = jnp.where(lane == TOP_K + k, lr, sel_t)
        gate_t = jnp.where(lane == k, es[k] / esum, gate_t)
    sel_o[...] = sel_t.astype(jnp.int32)
    selt_o[0] = sel_t.T[0:8, :].astype(jnp.int32)
    gate_o[...] = gate_t
    row8 = lax.broadcasted_iota(jnp.int32, (8, LANES), 0)
    meta = jnp.where(row8 == 0, carry_sc[...], jnp.where(row8 == 1, tile_cnt, 0.0))
    meta_o[0] = meta.astype(jnp.int32)
    carry_sc[...] = carry_sc[...] + jnp.ceil(tile_cnt * (1.0 / RUN_ALIGN)) * RUN_ALIGN


def _out_call(x_all, mod3, ya, yb, hf, hb, oc, yd, wo, g2, wr, br, n_batch):
    t_all, d = x_all.shape
    n_tiles = t_all // TILE
    tok, _, modr = _tile_maps(n_tiles // n_batch, n_batch)
    const = lambda i: (0, 0)
    full = lambda a: pl.BlockSpec(a.shape, const)
    t256 = pl.BlockSpec((TILE, 256), tok)
    td = pl.BlockSpec((TILE, d), tok)
    tl = pl.BlockSpec((TILE, LANES), tok)
    return pl.pallas_call(
        _out_kernel,
        grid=(n_tiles,),
        in_specs=[td, pl.BlockSpec((1, 1, mod3.shape[-1]), modr), t256, t256, t256, t256, t256, t256,
                  full(wo), full(g2), full(wr), full(br)],
        out_specs=[td, td, tl, pl.BlockSpec((1, 8, TILE), lambda i: (i, 0, 0)), tl,
                   pl.BlockSpec((1, 8, LANES), lambda i: (i, 0, 0))],
        out_shape=[jax.ShapeDtypeStruct((t_all, d), F32), jax.ShapeDtypeStruct((t_all, d), MXU),
                   jax.ShapeDtypeStruct((t_all, LANES), jnp.int32), jax.ShapeDtypeStruct((n_tiles, 8, TILE), jnp.int32),
                   jax.ShapeDtypeStruct((t_all, LANES), F32), jax.ShapeDtypeStruct((n_tiles, 8, LANES), jnp.int32)],
        scratch_shapes=[pltpu.VMEM((1, LANES), F32)],
        compiler_params=_cparams(("arbitrary",)),
        name="out_proj_router",
    )(x_all, mod3, ya, yb, hf, hb, oc, yd, wo, g2, wr, br)


SORT_R = 48
SORT_W = N_EXP * SORT_R
_M_CNT, _M_ROUNDS = N_EXP, 2 * N_EXP


def _route_plan(meta, t_all):
    base = meta[:, 0, :N_EXP]
    cnt = meta[:, 1, :N_EXP]
    total = base[-1] + (cnt[-1] + RUN_ALIGN - 1) // RUN_ALIGN * RUN_ALIGN
    padded = (total + SORT_R + MOE_BLK - 1) // MOE_BLK * MOE_BLK
    pad_end = jnp.cumsum(padded)
    pad_start = pad_end - padded
    rounds = jnp.maximum((jnp.max(cnt, axis=1, keepdims=True) + SORT_R - 1) // SORT_R, 1)
    plan = jnp.concatenate([pad_start[None, :] + base, cnt, rounds,
                            jnp.zeros((meta.shape[0], LANES - 2 * N_EXP - 1), jnp.int32)], axis=1)
    n_runs = meta.shape[0] * N_EXP
    n_blk = (t_all * TOP_K + n_runs * (RUN_ALIGN - 1) + N_EXP * (SORT_R + MOE_BLK - 1) + MOE_BLK - 1) // MOE_BLK
    blk_row = jnp.arange(n_blk, dtype=jnp.int32)[:, None] * MOE_BLK
    blk_exp = jnp.minimum(jnp.sum((blk_row >= pad_end[None, :]).astype(jnp.int32), axis=1), N_EXP - 1)
    n_used = (pad_end[-1] // MOE_BLK).astype(jnp.int32).reshape(1)
    last_rounds = jnp.maximum((cnt[-1] + SORT_R - 1) // SORT_R, 1)
    tail_start = pad_start + base[-1] + SORT_R * last_rounds
    tail = jnp.concatenate([tail_start, (pad_end - tail_start) // TAIL_ROWS,
                            pad_end[-1:], n_blk - pad_end[-1:] // MOE_BLK,
                            jnp.zeros((LANES - 2 * N_EXP - 2,), jnp.int32)]).astype(jnp.int32).reshape(1, 1, LANES)
    return plan.astype(jnp.int32)[:, None, :], tail, blk_exp.astype(jnp.int32), n_used, n_blk


U32 = jnp.uint32
_HI16 = 0xFFFF0000


def _pack_pairs(x):
    half = x.shape[1] // 2
    bits = lax.bitcast_convert_type(x, U32)
    return (bits[:, half:] & U32(_HI16)) | (bits[:, :half] >> 16)


def _unpack_pairs(w):
    lo = lax.bitcast_convert_type(w << 16, F32).astype(MXU)
    hi = lax.bitcast_convert_type(w & U32(_HI16), F32).astype(MXU)
    return jnp.concatenate([lo, hi], axis=1)


def _run_start(plan_ref, e, r):
    return pl.multiple_of(plan_ref[0, 0, e] + r * SORT_R, RUN_ALIGN)


def _extra_round_copies(plan_ref, r, make_copy):
    for go in (lambda c: c.start(), lambda c: c.wait()):
        for e in range(N_EXP):
            @pl.when(plan_ref[0, 0, _M_CNT + e] > r * SORT_R)
            def _():
                go(make_copy(e, _run_start(plan_ref, e, r)))


def _round_cols(sel_e, sel_lr, r):
    lo = r * SORT_R
    ok = (sel_lr >= lo) & (sel_lr < lo + SORT_R)
    return jnp.where(ok, sel_e * SORT_R + sel_lr - lo, -1)


TAIL_ROWS = 8


def _dispatch_kernel(plan_ref, tail_ref, selt_ref, h2_ref, xs_o, stage, zrows, sem, zsem, pending):
    i = pl.program_id(0)
    last = pl.num_programs(0) - 1
    slot = i % 2
    h2 = h2_ref[...]
    rows = lax.broadcasted_iota(jnp.int32, (TILE, TILE), 0)

    def run_copy(buf, e, dst):
        return pltpu.make_async_copy(stage.at[buf, pl.ds(e * SORT_R, SORT_R), :],
                                     xs_o.at[pl.ds(dst, SORT_R), :], sem.at[buf, e])

    def wait_round0(buf):
        for e in range(N_EXP):
            run_copy(buf, e, 0).wait()

    def sort_round(r):
        cols = [_round_cols(selt_ref[0, k:k + 1, :], selt_ref[0, TOP_K + k:TOP_K + k + 1, :], r)
                for k in range(TOP_K)]
        for c in range(SORT_W // TILE):
            hit = jnp.zeros((TILE, TILE), F32)
            for k in range(TOP_K):
                hit = jnp.where(rows + c * TILE == cols[k], 1.0, hit)
            stage[slot, c * TILE:(c + 1) * TILE, :] = _pack_pairs(_dot(hit.astype(MXU), h2))

    @pl.when(i == 0)
    def _():
        pending[0] = 0

    sort_round(0)

    def start_all(after_previous):
        for e in range(N_EXP):
            if after_previous:
                run_copy(1 - slot, e, 0).wait()
            run_copy(slot, e, _run_start(plan_ref, e, 0)).start(priority=e % 2)

    @pl.when(pending[0] == 1)
    def _():
        start_all(True)

    @pl.when(pending[0] == 0)
    def _():
        start_all(False)

    pending[0] = 1
    rounds = plan_ref[0, 0, _M_ROUNDS]

    @pl.when((rounds > 1) | (i == last))
    def _():
        wait_round0(slot)
        pending[0] = 0

    def extra_round(r, carry):
        sort_round(r)
        _extra_round_copies(plan_ref, r, lambda e, dst: run_copy(slot, e, dst))
        return carry

    lax.fori_loop(1, rounds, extra_round, 0)

    @pl.when(i == last)
    def _():
        zrows[...] = jnp.zeros(zrows.shape, zrows.dtype)
        for go in (lambda c: c.start(), lambda c: c.wait()):
            for e in range(N_EXP):
                def fill(t, carry, e=e):
                    dst = pl.multiple_of(tail_ref[0, 0, e] + t * TAIL_ROWS, TAIL_ROWS)
                    go(pltpu.make_async_copy(zrows.at[pl.ds(0, TAIL_ROWS), :], xs_o.at[pl.ds(dst, TAIL_ROWS), :], zsem))
                    return carry
                lax.fori_loop(0, tail_ref[0, 0, _M_CNT + e], fill, 0)

            def fill_block(t, carry):
                dst = pl.multiple_of(tail_ref[0, 0, _M_ROUNDS] + t * MOE_BLK, MOE_BLK)
                go(pltpu.make_async_copy(zrows, xs_o.at[pl.ds(dst, MOE_BLK), :], zsem))
                return carry
            lax.fori_loop(0, tail_ref[0, 0, _M_ROUNDS + 1], fill_block, 0)


def _dispatch_call(plan, tail, selt, h2, cap):
    t_all, d = h2.shape
    n_tiles = t_all // TILE
    return pl.pallas_call(
        _dispatch_kernel,
        grid=(n_tiles,),
        in_specs=[pl.BlockSpec((1, 1, LANES), lambda i: (i, 0, 0), memory_space=pltpu.SMEM),
                  pl.BlockSpec((1, 1, LANES), lambda i: (0, 0, 0), memory_space=pltpu.SMEM),
                  pl.BlockSpec((1, 8, TILE), lambda i: (i, 0, 0)),
                  pl.BlockSpec((TILE, d), lambda i: (i, 0))],
        out_specs=pl.BlockSpec(memory_space=pl.ANY),
        out_shape=jax.ShapeDtypeStruct((cap, d // 2), U32),
        scratch_shapes=[pltpu.VMEM((2, SORT_W, d // 2), U32), pltpu.VMEM((MOE_BLK, d // 2), U32),
                        pltpu.SemaphoreType.DMA((2, N_EXP)), pltpu.SemaphoreType.DMA(()), pltpu.SMEM((1,), jnp.int32)],
        compiler_params=_cparams(("arbitrary",)),
        name="moe_dispatch",
    )(plan, tail, selt, h2)


GU_GRP = 256


def _regroup_kernel(w_ref, o_ref):
    rr = lax.broadcasted_iota(jnp.int32, (GU_GRP, GU_GRP), 0)
    cc = lax.broadcasted_iota(jnp.int32, (GU_GRP, GU_GRP), 1)
    half = GU_GRP // 2
    perm = (rr == jnp.where(cc < half, 2 * cc, 2 * (cc - half) + 1)).astype(MXU)
    for s in range(w_ref.shape[-1] // GU_GRP):
        sl = slice(s * GU_GRP, (s + 1) * GU_GRP)
        o_ref[:, sl] = _dot(w_ref[:, sl].astype(MXU), perm).astype(o_ref.dtype)


def _regroup_call(w_gu):
    n_exp, d, n2 = w_gu.shape
    tn = 512
    spec = pl.BlockSpec((None, d, tn), lambda e, j: (e, 0, j))
    return pl.pallas_call(
        _regroup_kernel,
        grid=(n_exp, n2 // tn),
        in_specs=[spec],
        out_specs=spec,
        out_shape=jax.ShapeDtypeStruct((n_exp, d, n2), MXU),
        compiler_params=_cparams(("parallel", "parallel")),
        name="regroup_gate_up",
    )(w_gu)


def _regroup_bias(b_gu):
    n_exp, n2 = b_gu.shape
    b = b_gu.astype(F32).reshape(n_exp, n2 // GU_GRP, GU_GRP // 2, 2)
    return jnp.concatenate([b[..., 0], b[..., 1]], axis=-1).reshape(n_exp, 1, n2)


def _expert_kernel(be_ref, nu_ref, xs_ref, wgu_ref, bgu_ref, wd_ref, bd_ref, y_ref):
    n = pl.program_id(0)
    half = GU_GRP // 2

    @pl.when(n < nu_ref[0])
    def _():
        xb = _unpack_pairs(xs_ref[...])
        gu = _dot(xb, wgu_ref[...]) + bgu_ref[...]
        acts = []
        for s in range(gu.shape[-1] // GU_GRP):
            g = jnp.minimum(gu[:, s * GU_GRP:s * GU_GRP + half], SWIGLU_LIMIT)
            u = jnp.clip(gu[:, s * GU_GRP + half:(s + 1) * GU_GRP], -SWIGLU_LIMIT, SWIGLU_LIMIT)
            acts.append(((u + 1.0) * g * jax.nn.sigmoid(SWIGLU_ALPHA * g)).astype(MXU))
        y = _dot(jnp.concatenate(acts, axis=1), wd_ref[...]) + bd_ref[...]
        y_ref[...] = _pack_pairs(y.astype(jnp.bfloat16).astype(F32))

    @pl.when(n >= nu_ref[0])
    def _():
        y_ref[...] = jnp.zeros(y_ref.shape, y_ref.dtype)


def _expert_call(blk_exp, n_used, xs, wgu, bgu, wd, bd):
    cap, dw = xs.shape
    d = 2 * dw
    dff2 = wgu.shape[-1]
    n_blk = cap // MOE_BLK
    emap = lambda n, be, nu: (be[n], 0, 0)
    rows = lambda n, be, nu: (n, 0)
    used_rows = lambda n, be, nu: (jnp.where(n < nu[0], n, 0), 0)
    return pl.pallas_call(
        _expert_kernel,
        grid_spec=pltpu.PrefetchScalarGridSpec(
            num_scalar_prefetch=2,
            grid=(n_blk,),
            in_specs=[pl.BlockSpec((MOE_BLK, dw), used_rows),
                      pl.BlockSpec((None, d, dff2), emap), pl.BlockSpec((None, 1, dff2), emap),
                      pl.BlockSpec((None, dff2 // 2, d), emap), pl.BlockSpec((None, 1, d), emap)],
            out_specs=pl.BlockSpec((MOE_BLK, dw), rows)),
        out_shape=jax.ShapeDtypeStruct((cap, dw), U32),
        compiler_params=_cparams(("arbitrary",)),
        name="expert_ffn",
    )(blk_exp, n_used, xs, wgu, bgu, wd, bd)


COMBINE_AHEAD = 2


def _combine_kernel(plan_ref, next_ref, ahead_ref, sel_ref, gate_ref, x1_ref, mod_ref, y_hbm, x2_o, stage, sem,
                    gf_ref=None):
    d = x1_ref.shape[-1]
    i = pl.program_id(0)
    n_buf = COMBINE_AHEAD + 1
    slot = i % n_buf
    sel = sel_ref[...]
    gate = gate_ref[...]
    lanes = lax.broadcasted_iota(jnp.int32, (TILE, TILE), 1)

    def run_copy(buf, e, src):
        return pltpu.make_async_copy(y_hbm.at[pl.ds(src, SORT_R), :],
                                     stage.at[buf, pl.ds(e * SORT_R, SORT_R), :], sem.at[buf])

    def weighted_sum(r, acc):
        cols = [_round_cols(sel[:, k:k + 1], sel[:, TOP_K + k:TOP_K + k + 1], r) for k in range(TOP_K)]
        for c in range(SORT_W // TILE):
            w = jnp.zeros((TILE, TILE), F32)
            for k in range(TOP_K):
                w = jnp.where(lanes + c * TILE == cols[k], gate[:, k:k + 1], w)
            acc = acc + _dot(w.astype(MXU), _unpack_pairs(stage[slot, c * TILE:(c + 1) * TILE, :]))
        return acc

    def fetch(buf, pref):
        for e in range(N_EXP):
            run_copy(buf, e, _run_start(pref, e, 0)).start(priority=e % 2)

    @pl.when(i == 0)
    def _():
        fetch(0, plan_ref)

    @pl.when((i == 0) & (pl.num_programs(0) > 1))
    def _():
        fetch(1, next_ref)

    @pl.when(i + COMBINE_AHEAD < pl.num_programs(0))
    def _():
        fetch((i + COMBINE_AHEAD) % n_buf, ahead_ref)

    for e in range(N_EXP):
        run_copy(slot, e, 0).wait()
    f = weighted_sum(0, jnp.zeros((TILE, d), F32))

    def extra_round(r, acc):
        _extra_round_copies(plan_ref, r, lambda e, src: run_copy(slot, e, src))
        return weighted_sum(r, acc)

    f = lax.fori_loop(1, plan_ref[0, 0, _M_ROUNDS], extra_round, f)
    x2 = x1_ref[...] + mod_ref[0][:, 5 * d:6 * d] * f
    if gf_ref is None:
        x2_o[...] = x2
    else:
        x2_o[...] = _rms(x2) * gf_ref[...]


def _combine_final_kernel(plan_ref, next_ref, ahead_ref, sel_ref, gate_ref, x1_ref, mod_ref, gf_ref, y_hbm,
                          out_o, stage, sem):
    _combine_kernel(plan_ref, next_ref, ahead_ref, sel_ref, gate_ref, x1_ref, mod_ref, y_hbm, out_o, stage, sem,
                    gf_ref=gf_ref)


def _combine_call(plan, sel, gate, x1, mod3, y, n_batch, g_final=None):
    t_all, d = x1.shape
    n_tiles = t_all // TILE
    ntb = n_tiles // n_batch
    tok, _, modr = _tile_maps(ntb, n_batch)
    if g_final is None:
        body, extra_in, extra_args = _combine_kernel, [], ()
        out_spec = pl.BlockSpec((TILE, d), tok)
        out_shape = jax.ShapeDtypeStruct((t_all, d), F32)
    else:
        body, extra_in, extra_args = _combine_final_kernel, [pl.BlockSpec((1, d), lambda i: (0, 0))], (g_final,)
        out_spec = pl.BlockSpec((None, TILE, d), lambda i: (i // ntb, jnp.maximum(i % ntb - 1, 0), 0))
        out_shape = jax.ShapeDtypeStruct((n_batch, (ntb - 1) * TILE, d), F32)
    return _combine_pallas(body, extra_in, extra_args, out_spec, out_shape, plan, sel, gate, x1, mod3, y, tok, modr)


def _combine_pallas(body, extra_in, extra_args, out_spec, out_shape, plan, sel, gate, x1, mod3, y, tok, modr):
    t_all, d = x1.shape
    n_tiles = t_all // TILE
    return pl.pallas_call(
        body,
        grid=(n_tiles,),
        in_specs=[pl.BlockSpec((1, 1, LANES), lambda i: (i, 0, 0), memory_space=pltpu.SMEM),
                  pl.BlockSpec((1, 1, LANES), lambda i: (jnp.minimum(i + 1, n_tiles - 1), 0, 0),
                               memory_space=pltpu.SMEM),
                  pl.BlockSpec((1, 1, LANES), lambda i: (jnp.minimum(i + COMBINE_AHEAD, n_tiles - 1), 0, 0),
                               memory_space=pltpu.SMEM),
                  pl.BlockSpec((TILE, LANES), tok), pl.BlockSpec((TILE, LANES), tok), pl.BlockSpec((TILE, d), tok),
                  pl.BlockSpec((1, 1, mod3.shape[-1]), modr)] + extra_in + [pl.BlockSpec(memory_space=pl.ANY)],
        out_specs=out_spec,
        out_shape=out_shape,
        scratch_shapes=[pltpu.VMEM((COMBINE_AHEAD + 1, SORT_W, d // 2), U32),
                        pltpu.SemaphoreType.DMA((COMBINE_AHEAD + 1,))],
        compiler_params=_cparams(("arbitrary",)),
        name="moe_combine",
    )(plan, plan, plan, sel, gate, x1, mod3, *extra_args, y)


def kernel(x, c, ctx, c_ctx, w_ada, b_ada, g_norm1, g_norm2, w_in, attn_sink, nat_rpb, mlstm_conv, mlstm_gate_bias, mla_g_q, mla_w_uq, mla_g_kv, mla_w_ukv, w_out, w_router, b_router, w_gu, b_gu, w_down, b_down, g_final):
    n_batch, s_len, d = x.shape
    ctx_len = ctx.shape[1]
    assert ctx_len == TILE and s_len % TILE == 0 and s_len // GRID_W >= NAT_KR
    depth = w_ada.shape[0]
    npb = ctx_len + s_len
    c_all = jnp.concatenate([c, c_ctx[None], jnp.zeros((8 - n_batch - 1, d), F32)], axis=0)
    mod_all = _mod_call(c_all, w_ada, b_ada)
    tabs = _rope_tables(s_len, ctx_len)
    x_all = jnp.concatenate([ctx, x], axis=1).reshape(n_batch * npb, d)
    n_le = depth * N_EXP
    ew = (_regroup_call(w_gu.reshape((n_le,) + w_gu.shape[2:])), _regroup_bias(b_gu.reshape(n_le, -1)),
          w_down.reshape((n_le,) + w_down.shape[2:]).astype(MXU), b_down.reshape(n_le, 1, d).astype(F32))
    for l in range(depth):
        pk = _pack_layer(w_in[l], mlstm_gate_bias[l], mla_w_uq[l], mla_w_ukv[l], mla_g_q[l], mla_g_kv[l],
                         w_out[l], mlstm_conv[l], nat_rpb[l])
        mod3 = mod_all[l].reshape(8, 1, 6 * d)
        (qa, ka, va, qb, kb, vb, qk, vc, oc, gi, gf, qd, kd, vd) = _proj_call(
            x_all, mod3, g_norm1[l].reshape(1, d), pk, tabs, n_batch)
        ya = _mixa_call(qa, ka, va, attn_sink[l], n_batch, s_len)
        yb = _mixb_call(qb, kb, vb, _nat_bias_table(pk["rpb"]), n_batch, s_len)
        hf, hb = _mixc_call(qk, vc, gi, gf, pk["conv"], n_batch)
        yd = _mixd_call(qd, kd, vd, n_batch)
        last = l == depth - 1
        x_all = _ffn_layer(x_all, mod3, ya, yb, hf, hb, oc, yd, pk["wo"], g_norm2[l], w_router[l], b_router[l],
                           ew, l * N_EXP, n_batch, g_final.reshape(1, d).astype(F32) if last else None)
    return x_all


def _ffn_layer(x_all, mod3, ya, yb, hf, hb, oc, yd, wo, g2, w_router, b_router, ew, exp_off, n_batch, g_final):
    t_all, d = x_all.shape
    wr = jnp.concatenate([w_router.astype(F32), jnp.zeros((d, LANES - N_EXP), F32)], axis=1)
    br = jnp.concatenate([b_router.astype(F32), jnp.full((LANES - N_EXP,), NEG, F32)]).reshape(1, LANES)
    x1, h2, sel, selt, gate, meta = _out_call(x_all, mod3, ya, yb, hf, hb, oc, yd, wo, g2.reshape(1, d),
                                              wr, br, n_batch)
    plan, tail, blk_exp, n_used, n_blk = _route_plan(meta, t_all)
    xs = _dispatch_call(plan, tail, selt, h2, n_blk * MOE_BLK)
    y = _expert_call(blk_exp + exp_off, n_used, xs, *ew)
    return _combine_call(plan, sel, gate, x1, mod3, y, n_batch, g_final)
```
